```python
import math
import jax, jax.numpy as jnp
from jax import lax
import numpy as np

D_MODEL = 1024
BATCH = 32
SEQ = 2048
DEPTH = 4
DEC_BATCH = 16
DEC_SEQ = 2048
PAST_LEN = 128

HEAD_DIM = 64
GRID_W = 64
NA_HEADS = 4
NA_WIN_H = 8
NA_WIN_W = 16
SW_HEADS = 4
SW_KV_HEADS = 2
SW_HALF_WINDOW = 128
SW_BLOCK = 128
DIL_PAIRS = ((128, 1), (512, 4), (2048, 16))
DIL_HEADS_PER_GROUP = 2
DIL_HEADS = DIL_HEADS_PER_GROUP * len(DIL_PAIRS)
DIFF_HEADS = 4
DIFF_HALF = HEAD_DIM // 2
DIFF_Q_BLOCK = 128
T5_BUCKETS = 32
T5_MAX_DIST = 128
T5_B0 = 0
T5_C0 = T5_B0 + SW_HEADS
T5_D0 = T5_C0 + DIL_HEADS
T5_HEADS = T5_D0 + DIFF_HEADS
A_W = 3 * NA_HEADS * HEAD_DIM
B_W = (SW_HEADS + 2 * SW_KV_HEADS) * HEAD_DIM
C_W = 3 * DIL_HEADS * HEAD_DIM
D_W = 3 * DIFF_HEADS * HEAD_DIM
PROJ_WIDTH = A_W + B_W + C_W + D_W
SPLIT_POINTS = (A_W, A_W + B_W, A_W + B_W + C_W)
MIX_WIDTH = (NA_HEADS + SW_HEADS + DIL_HEADS + DIFF_HEADS) * HEAD_DIM
D_FF = 2816
RMS_EPS = 1e-6
NEG_INF = -1e30

kernel_name = 'hybrid_parallel_heads_bidir_encoder'


def rms_norm(x, g):
    xf = x.astype(jnp.float32)
    y = xf * lax.rsqrt(jnp.mean(xf * xf, axis=-1, keepdims=True) + RMS_EPS)
    return (y * g.astype(jnp.float32)).astype(x.dtype)


def t5_bucket(rel):
    nb = T5_BUCKETS // 2
    max_exact = nb // 2
    base = jnp.where(rel > 0, nb, 0)
    n = jnp.abs(rel)
    nf = jnp.maximum(n, 1).astype(jnp.float32)
    large = max_exact + (jnp.log(nf / max_exact) / math.log(T5_MAX_DIST / max_exact)
                         * (nb - max_exact)).astype(jnp.int32)
    large = jnp.minimum(large, nb - 1)
    return base + jnp.where(n < max_exact, n, large)


def banded_attention(q, k, v, half_window, block, stride, bias_table, sink):
    Bsz, L, H, d = q.shape
    Hk = k.shape[2]
    G = H // Hk
    nb = -(-L // block)
    pad = nb * block - L
    qb = jnp.pad(q, ((0, 0), (0, pad), (0, 0), (0, 0))).reshape(Bsz, nb, block, Hk, G, d)
    kp = jnp.pad(k, ((0, 0), (block, block + pad), (0, 0), (0, 0)))
    vp = jnp.pad(v, ((0, 0), (block, block + pad), (0, 0), (0, 0)))
    kidx = np.arange(nb)[:, None] * block + np.arange(3 * block)[None, :]
    kb = kp[:, kidx]
    vb = vp[:, kidx]
    qi = np.arange(nb)[:, None] * block + np.arange(block)[None, :]
    kj = kidx - block
    rel = kj[:, None, :] - qi[:, :, None]
    mask = (np.abs(rel) <= half_window) & (kj[:, None, :] >= 0) & (kj[:, None, :] < L)
    bias = bias_table.astype(jnp.float32)[t5_bucket(jnp.asarray(rel * stride, dtype=jnp.int32))]
    bias = bias.transpose(0, 3, 1, 2).reshape(nb, Hk, G, block, 3 * block)
    s = jnp.einsum('bnqhgd,bnkhd->bnhgqk', qb, kb, preferred_element_type=jnp.float32)
    s = s * (d ** -0.5) + bias[None]
    s = jnp.where(mask[None, :, None, None], s, NEG_INF)
    m = jnp.max(s, axis=-1, keepdims=True)
    if sink is not None:
        sk = sink.astype(jnp.float32).reshape(1, 1, Hk, G, 1, 1)
        m = jnp.maximum(m, sk)
        p = jnp.exp(s - m)
        denom = jnp.sum(p, axis=-1, keepdims=True) + jnp.exp(sk - m)
    else:
        p = jnp.exp(s - m)
        denom = jnp.sum(p, axis=-1, keepdims=True)
    o = jnp.einsum('bnhgqk,bnkhd->bnhgqd', p, vb.astype(jnp.float32)) / denom
    lse = (m + jnp.log(denom))[..., 0]
    o = o.transpose(0, 1, 4, 2, 3, 5).reshape(Bsz, nb * block, H, d)[:, :L]
    lse = lse.transpose(0, 1, 4, 2, 3).reshape(Bsz, nb * block, H)[:, :L]
    return o, lse


def neighborhood_attention(proj_a, qkn, rpb):
    Bsz, N, _ = proj_a.shape
    H, d = NA_HEADS, HEAD_DIM
    qkv = proj_a.reshape(Bsz, N, 3, H, d)
    q = rms_norm(qkv[:, :, 0], qkn[0])
    k = rms_norm(qkv[:, :, 1], qkn[1])
    v = qkv[:, :, 2]
    rows = N // GRID_W
    kh = min(NA_WIN_H, rows)
    n_cb = GRID_W // NA_WIN_W
    kb_w = 2 * NA_WIN_W
    r = np.arange(rows)
    rs = np.clip(r - kh // 2, 0, rows - kh)
    key_rows = rs[:, None] + np.arange(kh)[None, :]
    dr_idx = key_rows - r[:, None] + NA_WIN_H - 1
    j = np.arange(n_cb)
    kbs = np.clip(j * NA_WIN_W - NA_WIN_W // 2, 0, GRID_W - kb_w)
    key_cols = kbs[:, None] + np.arange(kb_w)[None, :]
    qc = j[:, None] * NA_WIN_W + np.arange(NA_WIN_W)[None, :]
    cs = np.clip(qc - NA_WIN_W // 2, 0, GRID_W - NA_WIN_W)
    col_ok = (key_cols[:, None, :] >= cs[:, :, None]) & (key_cols[:, None, :] < cs[:, :, None] + NA_WIN_W)
    dc_idx = np.clip(key_cols[:, None, :] - qc[:, :, None] + NA_WIN_W - 1, 0, 2 * NA_WIN_W - 2)
    n_k = kh * kb_w
    tok = (key_rows[:, None, :, None] * GRID_W + key_cols[None, :, None, :]).reshape(rows, n_cb, n_k)
    mask = np.broadcast_to(col_ok[:, :, None, :], (n_cb, NA_WIN_W, kh, kb_w)).reshape(n_cb, NA_WIN_W, n_k)
    bias = rpb.astype(jnp.float32)[:, dr_idx[:, None, None, :, None], dc_idx[None, :, :, None, :]]
    bias = bias.reshape(H, rows, n_cb, NA_WIN_W, n_k)
    qb = q.reshape(Bsz, rows, n_cb, NA_WIN_W, H, d)
    kb = k[:, tok]
    vb = v[:, tok]
    s = jnp.einsum('brjqhd,brjkhd->bhrjqk', qb, kb, preferred_element_type=jnp.float32)
    s = jnp.where(mask, s * (d ** -0.5) + bias[None], NEG_INF)
    p = jax.nn.softmax(s, axis=-1)
    o = jnp.einsum('bhrjqk,brjkhd->brjqhd', p, vb.astype(jnp.float32))
    return o.reshape(Bsz, N, H * d)


def sliding_window_gqa(proj_b, qkn, sink, t5_heads):
    Bsz, N, _ = proj_b.shape
    d = HEAD_DIM
    qw = SW_HEADS * d
    kw = SW_KV_HEADS * d
    q = rms_norm(proj_b[..., :qw].reshape(Bsz, N, SW_HEADS, d), qkn[0])
    k = rms_norm(proj_b[..., qw:qw + kw].reshape(Bsz, N, SW_KV_HEADS, d), qkn[1])
    v = proj_b[..., qw + kw:].reshape(Bsz, N, SW_KV_HEADS, d)
    o, _ = banded_attention(q, k, v, SW_HALF_WINDOW, SW_BLOCK, 1, t5_heads, sink)
    return o.reshape(Bsz, N, qw)


def dilated_attention(proj_c, qkn, t5_heads):
    Bsz, N, _ = proj_c.shape
    H, d, G = DIL_HEADS, HEAD_DIM, DIL_HEADS_PER_GROUP
    qkv = proj_c.reshape(Bsz, N, 3, H, d)
    q = rms_norm(qkv[:, :, 0], qkn[0])
    k = rms_norm(qkv[:, :, 1], qkn[1])
    v = qkv[:, :, 2]
    outs, lses = [], []
    for g, (w, r) in enumerate(DIL_PAIRS):
        L = N // r
        hw = w // (2 * r)
        hs = slice(g * G, (g + 1) * G)

        def to_sub(t):
            return t.reshape(Bsz, L, r, G, d).transpose(0, 2, 1, 3, 4).reshape(Bsz * r, L, G, d)

        o, lse = banded_attention(to_sub(q[:, :, hs]), to_sub(k[:, :, hs]), to_sub(v[:, :, hs]),
                                  hw, hw, r, t5_heads[:, hs], None)
        outs.append(o.reshape(Bsz, r, L, G, d).transpose(0, 2, 1, 3, 4).reshape(Bsz, N, G, d))
        lses.append(lse.reshape(Bsz, r, L, G).transpose(0, 2, 1, 3).reshape(Bsz, N, G))
    alpha = jax.nn.softmax(jnp.stack(lses, axis=0), axis=0)
    o = jnp.concatenate([alpha[g][..., None] * outs[g] for g in range(len(DIL_PAIRS))], axis=2)
    return o.reshape(Bsz, N, H * d)


def differential_attention(proj_d, qkn, lam_p, subln_g, t5_heads, lambda_init):
    Bsz, N, _ = proj_d.shape
    H, e, d = DIFF_HEADS, DIFF_HALF, HEAD_DIM
    qkv = proj_d.reshape(Bsz, N, 3, H, d)
    q = rms_norm(qkv[:, :, 0].reshape(Bsz, N, H, 2, e), qkn[0])
    k = rms_norm(qkv[:, :, 1].reshape(Bsz, N, H, 2, e), qkn[1])
    v = qkv[:, :, 2].astype(jnp.float32)
    lp = lam_p.astype(jnp.float32)
    lam = jnp.exp(jnp.sum(lp[0] * lp[1])) - jnp.exp(jnp.sum(lp[2] * lp[3])) + lambda_init
    nqb = N // DIFF_Q_BLOCK
    q_blocks = q.reshape(Bsz, nqb, DIFF_Q_BLOCK, H, 2, e).transpose(1, 0, 2, 3, 4, 5)
    kpos = jnp.arange(N, dtype=jnp.int32)
    table = t5_heads.astype(jnp.float32)

    def one_block(args):
        qb, q0 = args
        qpos = q0 + jnp.arange(DIFF_Q_BLOCK, dtype=jnp.int32)
        bias = table[t5_bucket(kpos[None, :] - qpos[:, None])].transpose(2, 0, 1)
        s = jnp.einsum('bqhie,bkhie->bihqk', qb, k, preferred_element_type=jnp.float32)
        p = jax.nn.softmax(s * (e ** -0.5) + bias[None, None], axis=-1)
        a = p[:, 0] - lam * p[:, 1]
        return jnp.einsum('bhqk,bkhd->bqhd', a, v)

    o = lax.map(one_block, (q_blocks, jnp.arange(nqb, dtype=jnp.int32) * DIFF_Q_BLOCK))
    o = o.transpose(1, 0, 2, 3, 4).reshape(Bsz, N, H, d)
    o = rms_norm(o, subln_g) * (1.0 - lambda_init)
    return o.reshape(Bsz, N, H * d)


def conv_glu_ffn(h, w_up, conv_w, conv_b, w_down):
    u = h @ w_up
    up = jnp.pad(u, ((0, 0), (1, 1), (0, 0)))
    u = up[:, :-2] * conv_w[0] + up[:, 1:-1] * conv_w[1] + up[:, 2:] * conv_w[2] + conv_b
    val, gate = jnp.split(u, 2, axis=-1)
    return (val * jax.nn.silu(gate)) @ w_down


def _trunk(x, c, norm_attn_g, norm_ffn_g, w_ada, b_ada, w_in, qkn_a, qkn_b, qkn_c, qkn_d,
           rpb_a, sink_b, t5_table, lam_d, subln_d, w_out, w_up, conv_w, conv_b, w_down):
    Bsz, N, D = x.shape
    for l in range(DEPTH):
        mod = (jax.nn.silu(c) @ w_ada[l] + b_ada[l]).reshape(Bsz, 6, 1, D)
        shift_a, scale_a, gate_a = mod[:, 0], mod[:, 1], mod[:, 2]
        shift_f, scale_f, gate_f = mod[:, 3], mod[:, 4], mod[:, 5]
        h = rms_norm(x, norm_attn_g[l]) * (1.0 + scale_a) + shift_a
        proj = h @ w_in[l]
        pa, pb, pc, pd = jnp.split(proj, SPLIT_POINTS, axis=-1)
        lambda_init = 0.8 - 0.6 * math.exp(-0.3 * l)
        o_a = neighborhood_attention(pa, qkn_a[l], rpb_a[l])
        o_b = sliding_window_gqa(pb, qkn_b[l], sink_b[l], t5_table[:, T5_B0:T5_C0])
        o_c = dilated_attention(pc, qkn_c[l], t5_table[:, T5_C0:T5_D0])
        o_d = differential_attention(pd, qkn_d[l], lam_d[l], subln_d[l], t5_table[:, T5_D0:T5_HEADS], lambda_init)
        mixed = jnp.concatenate([o_a, o_b, o_c, o_d], axis=-1).astype(x.dtype)
        x = x + gate_a * (mixed @ w_out[l])
        h = rms_norm(x, norm_ffn_g[l]) * (1.0 + scale_f) + shift_f
        x = x + gate_f * conv_glu_ffn(h, w_up[l], conv_w[l], conv_b[l], w_down[l])
    return x


def setup_inputs(seed: int = 0) -> dict:
    key = jax.random.key(seed)
    ks = jax.random.split(key, 24)
    D = D_MODEL

    def nrm(k, shape, scale):
        return jax.random.normal(k, shape, jnp.float32) * scale

    return {
        'x_prompt': nrm(ks[0], (BATCH, SEQ, D), 1.0),
        'x_sample': nrm(ks[1], (DEC_BATCH, DEC_SEQ, D), 1.0),
        'c_prompt': nrm(ks[2], (BATCH, D), 1.0),
        'c_sample': nrm(ks[3], (DEC_BATCH, D), 1.0),
        'norm_attn_g': 1.0 + nrm(ks[4], (DEPTH, D), 0.05),
        'norm_ffn_g': 1.0 + nrm(ks[5], (DEPTH, D), 0.05),
        'w_ada': nrm(ks[6], (DEPTH, D, 6 * D), 0.5 * D ** -0.5),
        'b_ada': nrm(ks[7], (DEPTH, 6 * D), 0.02),
        'w_in': nrm(ks[8], (DEPTH, D, PROJ_WIDTH), D ** -0.5),
        'qkn_a': 1.0 + nrm(ks[9], (DEPTH, 2, HEAD_DIM), 0.05),
        'qkn_b': 1.0 + nrm(ks[10], (DEPTH, 2, HEAD_DIM), 0.05),
        'qkn_c': 1.0 + nrm(ks[11], (DEPTH, 2, HEAD_DIM), 0.05),
        'qkn_d': 1.0 + nrm(ks[12], (DEPTH, 2, DIFF_HALF), 0.05),
        'rpb_a': nrm(ks[13], (DEPTH, NA_HEADS, 2 * NA_WIN_H - 1, 2 * NA_WIN_W - 1), 0.1),
        'sink_b': nrm(ks[14], (DEPTH, SW_HEADS), 0.5),
        't5_table': nrm(ks[15], (T5_BUCKETS, T5_HEADS), 0.1),
        'lam_d': nrm(ks[16], (DEPTH, 4, DIFF_HALF), 0.1),
        'subln_d': 1.0 + nrm(ks[17], (DEPTH, HEAD_DIM), 0.05),
        'w_out': nrm(ks[18], (DEPTH, MIX_WIDTH, D), MIX_WIDTH ** -0.5),
        'w_up': nrm(ks[19], (DEPTH, D, 2 * D_FF), D ** -0.5),
        'conv_w': nrm(ks[20], (DEPTH, 3, 2 * D_FF), 0.3) + jnp.array([0.0, 1.0, 0.0], jnp.float32)[None, :, None],
        'conv_b': nrm(ks[21], (DEPTH, 2 * D_FF), 0.02),
        'w_down': nrm(ks[22], (DEPTH, D_FF, D), D_FF ** -0.5),
    }


def reference(x_prompt, x_sample, c_prompt, c_sample, norm_attn_g, norm_ffn_g, w_ada, b_ada, w_in,
              qkn_a, qkn_b, qkn_c, qkn_d, rpb_a, sink_b, t5_table, lam_d, subln_d, w_out, w_up,
              conv_w, conv_b, w_down):
    y_prompt = _trunk(x_prompt, c_prompt, norm_attn_g, norm_ffn_g, w_ada, b_ada, w_in, qkn_a, qkn_b,
                      qkn_c, qkn_d, rpb_a, sink_b, t5_table, lam_d, subln_d, w_out, w_up, conv_w,
                      conv_b, w_down)
    y_sample = _trunk(x_sample, c_sample, norm_attn_g, norm_ffn_g, w_ada, b_ada, w_in, qkn_a, qkn_b,
                      qkn_c, qkn_d, rpb_a, sink_b, t5_table, lam_d, subln_d, w_out, w_up, conv_w,
                      conv_b, w_down)
    return (y_prompt, y_sample)
```

```python
import functools
import math

import numpy as np
import jax
import jax.numpy as jnp
from jax import lax
from jax.experimental import pallas as pl
from jax.experimental.pallas import tpu as pltpu

D_MODEL = 1024
SEQ = 2048
DEPTH = 4
HEAD_DIM = 64
GRID_W = 64
NA_HEADS = 4
NA_WIN_H = 8
NA_WIN_W = 16
SW_HEADS = 4
SW_KV_HEADS = 2
SW_HALF_WINDOW = 128
DIL_PAIRS = ((128, 1), (512, 4), (2048, 16))
DIFF_HEADS = 4
DIFF_HALF = HEAD_DIM // 2
T5_BUCKETS = 32
T5_MAX_DIST = 128
T5_B0 = 0
T5_C0 = 4
T5_D0 = 10
D_FF = 2816
RMS_EPS = 1e-6
NEG_INF = -1e30
LOG2E = 1.4426950408889634

F32 = jnp.float32
BF16 = jnp.bfloat16

VMEM_LIMIT_BYTES = 56 * 1024 * 1024
TN_IN = 512
TN_OUT = 512
TN_FFN = 512
FF_CHUNK = 256
N_FF_CHUNKS = D_FF // FF_CHUNK
HALO = 8
TQ_D = 256
KB_D = 256
TQ_B = 128
TQ_C = 128
A_PAIR = 2 * GRID_W
A_WIN = 10 * GRID_W

QT_ROWS = 768
VT_ROWS = 640
K_ROWS = 640
C_QK_ROWS = 768
WT_ROWS = QT_ROWS + VT_ROWS + K_ROWS + C_QK_ROWS
C_WIDTH = 1152


def _cparams(sem):
    return pltpu.CompilerParams(dimension_semantics=sem, vmem_limit_bytes=VMEM_LIMIT_BYTES)


def _const_spec(shape):
    nd = len(shape)
    return pl.BlockSpec(shape, lambda *_: (0,) * nd, pipeline_mode=pl.Buffered(1))


def _ada_kernel(c_ref, w_ref, b_ref, o_ref):
    c = c_ref[...]
    sc = (c * (1.0 / (1.0 + jnp.exp(-c)))).astype(BF16)
    o_ref[0] = jnp.dot(sc, w_ref[0].astype(BF16), preferred_element_type=F32) + b_ref[0]


def _ada_call(c, w_ada, b_ada):
    bt = c.shape[0]
    ncol = 6 * D_MODEL
    cb = 1024
    return pl.pallas_call(
        _ada_kernel,
        grid=(DEPTH, ncol // cb),
        in_specs=[
            pl.BlockSpec((bt, D_MODEL), lambda l, j: (0, 0)),
            pl.BlockSpec((1, D_MODEL, cb), lambda l, j: (l, 0, j)),
            pl.BlockSpec((1, 1, cb), lambda l, j: (l, 0, j)),
        ],
        out_specs=pl.BlockSpec((1, bt, cb), lambda l, j: (l, 0, j)),
        out_shape=jax.ShapeDtypeStruct((DEPTH, bt, ncol), F32),
        compiler_params=_cparams(("arbitrary", "arbitrary")),
        name="ada_mod",
    )(c, w_ada, b_ada.reshape(DEPTH, 1, ncol))


def _mod_norm(x, g, scale, shift):
    ms = jnp.mean(x * x, axis=-1, keepdims=True)
    return x * lax.rsqrt(ms + RMS_EPS) * (g * (1.0 + scale)) + shift


def _in_kernel(x_ref, mod_ref, g_ref, wt_ref, wvc_ref, gs_ref, qT_ref, vT_ref, k_ref, c_ref, h_scr):
    tn = x_ref.shape[1]
    h = _mod_norm(x_ref[0], g_ref[...], mod_ref[0, 1:2, :], mod_ref[0, 0:1, :])
    h_scr[...] = h.astype(BF16)
    hb = h_scr[...]

    def proj_t(r0, rows):
        return lax.dot_general(wt_ref[r0:r0 + rows, :], hb, (((1,), (1,)), ((), ())),
                               preferred_element_type=F32)

    def head_norm(y, r0, grp):
        rows = y.shape[0]
        y3 = y.reshape(rows // grp, grp, tn)
        ms = jnp.sum(y3 * y3, axis=1, keepdims=True) * (1.0 / grp)
        gs = gs_ref[r0:r0 + rows, :].reshape(rows // grp, grp, 1)
        return (y3 * lax.rsqrt(ms + RMS_EPS) * gs).reshape(rows, tn)

    for r0, grp in ((0, HEAD_DIM), (256, HEAD_DIM), (512, DIFF_HALF)):
        qT_ref[0, r0:r0 + 256, :] = head_norm(proj_t(r0, 256), r0, grp).astype(BF16)
    base = QT_ROWS
    for r0, rows in ((0, 256), (256, 256), (512, 128)):
        vT_ref[0, r0:r0 + rows, :] = proj_t(base + r0, rows).astype(BF16)
    base = QT_ROWS + VT_ROWS
    for r0, rows, grp in ((0, 256, DIFF_HALF), (256, 256, HEAD_DIM), (512, 128, HEAD_DIM)):
        kt = head_norm(proj_t(base + r0, rows), base + r0, grp)
        k_ref[0, :, r0:r0 + rows] = kt.T.astype(BF16)
    base = QT_ROWS + VT_ROWS + K_ROWS
    for r0 in (0, 256, 512):
        ct = head_norm(proj_t(base + r0, 256), base + r0, HEAD_DIM)
        c_ref[0, :, r0:r0 + 256] = ct.T
    c_ref[0, :, C_QK_ROWS:C_WIDTH] = jnp.dot(hb, wvc_ref[...], preferred_element_type=F32)


def _in_call(x, mod, g, wt, wvc, gs):
    bt, n, d = x.shape
    tn = TN_IN
    return pl.pallas_call(
        _in_kernel,
        grid=(bt, n // tn),
        in_specs=[
            pl.BlockSpec((1, tn, d), lambda b, t: (b, t, 0)),
            pl.BlockSpec((1, 6, d), lambda b, t: (b, 0, 0)),
            _const_spec((1, d)),
            _const_spec((WT_ROWS, d)),
            _const_spec((d, C_WIDTH - C_QK_ROWS)),
            _const_spec((WT_ROWS, 1)),
        ],
        out_specs=[
            pl.BlockSpec((1, QT_ROWS, tn), lambda b, t: (b, 0, t)),
            pl.BlockSpec((1, VT_ROWS, tn), lambda b, t: (b, 0, t)),
            pl.BlockSpec((1, tn, K_ROWS), lambda b, t: (b, t, 0)),
            pl.BlockSpec((1, tn, C_WIDTH), lambda b, t: (b, t, 0)),
        ],
        out_shape=[
            jax.ShapeDtypeStruct((bt, QT_ROWS, n), BF16),
            jax.ShapeDtypeStruct((bt, VT_ROWS, n), BF16),
            jax.ShapeDtypeStruct((bt, n, K_ROWS), BF16),
            jax.ShapeDtypeStruct((bt, n, C_WIDTH), F32),
        ],
        scratch_shapes=[pltpu.VMEM((tn, d), BF16)],
        compiler_params=_cparams(("parallel", "parallel")),
        name="in_proj",
    )(x, mod, g, wt, wvc, gs)


def _d_kernel(lam_ref, sub_ref, q_ref, k_ref, v_ref, tbl_ref, o_ref, s_scr, p_scr, og_scr, *,
              lambda_init):
    t = pl.program_id(1)
    tq = q_ref.shape[2]
    n = k_ref.shape[1]
    nkb = n // KB_D
    qT = q_ref[0]
    rowgrp = lax.broadcasted_iota(jnp.int32, qT.shape, 0) // DIFF_HALF

    def group_body(g, carry):
        h = g // 2
        qg = jnp.where(rowgrp == g, qT, jnp.zeros_like(qT))
        m8 = jnp.full((8, tq), -jnp.inf, F32)
        for jb in range(nkb):
            idx = jnp.clip(jb - t, -2, 2) + 2
            s = jnp.dot(k_ref[0, jb * KB_D:(jb + 1) * KB_D, :], qg, preferred_element_type=F32)
            s = s + tbl_ref[h, idx]
            s_scr[jb] = s
            m8 = jnp.maximum(m8, jnp.max(s.reshape(KB_D // 8, 8, tq), axis=0))
        m = jnp.max(m8, axis=0, keepdims=True)
        l8 = jnp.zeros((8, tq), F32)
        for jb in range(nkb):
            p = jnp.exp2(s_scr[jb] - m)
            l8 = l8 + jnp.sum(p.reshape(KB_D // 8, 8, tq), axis=0)
            p_scr[jb * KB_D:(jb + 1) * KB_D, :] = p.astype(BF16)
        l = jnp.sum(l8, axis=0, keepdims=True)
        vh = v_ref[0, pl.ds(pl.multiple_of(h * HEAD_DIM, HEAD_DIM), HEAD_DIM), :]
        og_scr[g] = jnp.dot(vh, p_scr[...], preferred_element_type=F32) * (1.0 / l)
        return carry

    lax.fori_loop(0, 2 * DIFF_HEADS, group_body, 0)

    lp = lam_ref[...]
    lam = (jnp.exp(jnp.sum(lp[0:1] * lp[1:2], axis=-1, keepdims=True))
           - jnp.exp(jnp.sum(lp[2:3] * lp[3:4], axis=-1, keepdims=True)) + lambda_init)
    parts = []
    for h in range(DIFF_HEADS):
        oh = og_scr[2 * h] - lam * og_scr[2 * h + 1]
        ms = jnp.mean(oh * oh, axis=0, keepdims=True)
        parts.append(oh * lax.rsqrt(ms + RMS_EPS) * sub_ref[...] * (1.0 - lambda_init))
    o_ref[0] = jnp.concatenate(parts, axis=0).T.astype(BF16)


def _d_call(lam_p, subln, qT, k, vT, tbl, lambda_init):
    bt, _, n = qT.shape
    tq = TQ_D
    return pl.pallas_call(
        functools.partial(_d_kernel, lambda_init=lambda_init),
        grid=(bt, n // tq),
        in_specs=[
            _const_spec((4, DIFF_HALF)),
            _const_spec((HEAD_DIM, 1)),
            pl.BlockSpec((1, 256, tq), lambda b, t: (b, 2, t)),
            pl.BlockSpec((1, n, 256), lambda b, t: (b, 0, 0)),
            pl.BlockSpec((1, 256, n), lambda b, t: (b, 0, 0)),
            _const_spec((DIFF_HEADS, 5, KB_D, tq)),
        ],
        out_specs=pl.BlockSpec((1, tq, 256), lambda b, t: (b, t, 0)),
        out_shape=jax.ShapeDtypeStruct((bt, n, 256), BF16),
        scratch_shapes=[
            pltpu.VMEM((n // KB_D, KB_D, tq), F32),
            pltpu.VMEM((n, tq), BF16),
            pltpu.VMEM((2 * DIFF_HEADS, HEAD_DIM, tq), F32),
        ],
        compiler_params=_cparams(("parallel", "arbitrary")),
        name="mixer_d",
    )(lam_p, subln, qT, k, vT, tbl)


def _b_kernel(sink_ref, q_ref, k_ref, v_ref, bias_ref, o_ref):
    n = k_ref.shape[1]
    tq = TQ_B
    nt = n // tq
    grp = SW_HEADS // SW_KV_HEADS

    def tile_body(t, carry):
        q0 = pl.multiple_of(t * tq, tq)
        qT = q_ref[0, :, pl.ds(q0, tq)]
        zero = jnp.zeros((HEAD_DIM, tq), BF16)
        outs = []
        for h in range(SW_HEADS):
            kv = h // grp
            qh = qT[h * HEAD_DIM:(h + 1) * HEAD_DIM]
            qsel = jnp.concatenate([qh, zero] if kv == 0 else [zero, qh], axis=0)
            sink = sink_ref[h] * LOG2E
            ss, starts = [], []
            m = jnp.full((1, tq), sink, F32)
            for j in (-1, 0, 1):
                jc = jnp.clip(t + j, 0, nt - 1)
                k0 = pl.multiple_of(jc * tq, tq)
                valid = jnp.logical_and(t + j >= 0, t + j <= nt - 1)
                s = jnp.dot(k_ref[0, pl.ds(k0, tq), :], qsel, preferred_element_type=F32)
                s = s + jnp.where(valid, bias_ref[h, j + 1], NEG_INF)
                m = jnp.maximum(m, jnp.max(s, axis=0, keepdims=True))
                ss.append(s)
                starts.append(k0)
            l = jnp.exp2(sink - m)
            o = jnp.zeros((HEAD_DIM, tq), F32)
            for s, k0 in zip(ss, starts):
                p = jnp.exp2(s - m)
                l = l + jnp.sum(p, axis=0, keepdims=True)
                vh = v_ref[0, kv * HEAD_DIM:(kv + 1) * HEAD_DIM, pl.ds(k0, tq)]
                o = o + jnp.dot(vh, p.astype(BF16), preferred_element_type=F32)
            outs.append(o * (1.0 / l))
        o_ref[0, pl.ds(q0, tq), :] = jnp.concatenate(outs, axis=0).T.astype(BF16)
        return carry

    lax.fori_loop(0, nt, tile_body, 0)


def _b_call(sink, qT, k, vT, bias):
    bt, _, n = qT.shape
    return pl.pallas_call(
        _b_kernel,
        grid=(bt,),
        in_specs=[
            pl.BlockSpec(memory_space=pltpu.SMEM),
            pl.BlockSpec((1, 256, n), lambda b: (b, 1, 0)),
            pl.BlockSpec((1, n, 128), lambda b: (b, 0, 4)),
            pl.BlockSpec((1, 128, n), lambda b: (b, 4, 0)),
            _const_spec((SW_HEADS, 3, TQ_B, TQ_B)),
        ],
        out_specs=pl.BlockSpec((1, n, 256), lambda b: (b, 0, 0)),
        out_shape=jax.ShapeDtypeStruct((bt, n, 256), BF16),
        compiler_params=_cparams(("parallel",)),
        name="mixer_b",
    )(sink, qT, k, vT, bias)


def _a_kernel(q_ref, k_ref, v_ref, bias_ref, o_ref):
    n = k_ref.shape[1]
    rows = n // GRID_W
    npair = n // A_PAIR
    kh = min(NA_WIN_H, rows)

    def pair_body(rp, carry):
        r = 2 * rp
        rs = jnp.clip(r - kh // 2, 0, rows - kh)
        ws = jnp.minimum(rs, rows - A_WIN // GRID_W)
        var = jnp.where(rp < 2, rp, jnp.where(rp < npair - 2, 2, rp - (npair - 5)))
        q0 = pl.multiple_of(rp * A_PAIR, A_PAIR)
        k0 = pl.multiple_of(ws * GRID_W, A_PAIR)
        qT = q_ref[0, :, pl.ds(q0, A_PAIR)]
        kw = k_ref[0, pl.ds(k0, A_WIN), :]
        rowhead = lax.broadcasted_iota(jnp.int32, qT.shape, 0) // HEAD_DIM
        outs = []
        for h in range(NA_HEADS):
            qsel = jnp.where(rowhead == h, qT, jnp.zeros_like(qT))
            s = jnp.dot(kw, qsel, preferred_element_type=F32) + bias_ref[h, var]
            m = jnp.max(s, axis=0, keepdims=True)
            p = jnp.exp2(s - m)
            l = jnp.sum(p, axis=0, keepdims=True)
            vh = v_ref[0, h * HEAD_DIM:(h + 1) * HEAD_DIM, pl.ds(k0, A_WIN)]
            outs.append(jnp.dot(vh, p.astype(BF16), preferred_element_type=F32) * (1.0 / l))
        o_ref[0, pl.ds(q0, A_PAIR), :] = jnp.concatenate(outs, axis=0).T.astype(BF16)
        return carry

    lax.fori_loop(0, npair, pair_body, 0)


def _a_call(qT, k, vT, bias):
    bt, _, n = qT.shape
    return pl.pallas_call(
        _a_kernel,
        grid=(bt,),
        in_specs=[
            pl.BlockSpec((1, 256, n), lambda b: (b, 0, 0)),
            pl.BlockSpec((1, n, 256), lambda b: (b, 0, 1)),
            pl.BlockSpec((1, 256, n), lambda b: (b, 1, 0)),
            _const_spec((NA_HEADS, 5, A_WIN, A_PAIR)),
        ],
        out_specs=pl.BlockSpec((1, n, 256), lambda b: (b, 0, 0)),
        out_shape=jax.ShapeDtypeStruct((bt, n, 256), BF16),
        compiler_params=_cparams(("parallel",)),
        name="mixer_a",
    )(qT, k, vT, bias)


def _c_kernel(q_ref, k_ref, v_ref, bias_ref, o_ref, lse_ref, sub_scr, o_scr, lse_scr, *, dil):
    n = q_ref.shape[1]
    sub_len = n // dil
    win = min(2 * TQ_C, sub_len)
    nqb = sub_len // TQ_C
    lane_head = lax.broadcasted_iota(jnp.int32, (TQ_C, 2 * HEAD_DIM), 1) // HEAD_DIM

    def sub_body(p, carry):
        rows = pl.ds(p, sub_len, stride=dil) if dil > 1 else pl.ds(0, sub_len)
        sub_scr[0] = q_ref[0, rows, :].astype(BF16)
        sub_scr[1] = k_ref[0, rows, :].astype(BF16)
        sub_scr[2] = v_ref[0, rows, :].astype(BF16)

        def q_body(qb, c2):
            q0 = pl.multiple_of(qb * TQ_C, TQ_C)
            start = pl.multiple_of(jnp.clip(q0 - TQ_C // 2, 0, sub_len - win), TQ_C // 2)
            var = jnp.where(qb == 0, 0, jnp.where(qb == nqb - 1, 2, 1))
            qblk = sub_scr[0, pl.ds(q0, TQ_C), :]
            kw = sub_scr[1, pl.ds(start, win), :]
            vw = sub_scr[2, pl.ds(start, win), :]
            o_h, lse_h = [], []
            for j in range(2):
                qm = jnp.where(lane_head == j, qblk, jnp.zeros_like(qblk))
                s = lax.dot_general(qm, kw, (((1,), (1,)), ((), ())), preferred_element_type=F32)
                s = s + bias_ref[j, var]
                m = jnp.max(s, axis=-1, keepdims=True)
                pr = jnp.exp2(s - m)
                l = jnp.sum(pr, axis=-1, keepdims=True)
                o_h.append(jnp.dot(pr.astype(BF16), vw, preferred_element_type=F32) * (1.0 / l))
                lse_h.append(jnp.broadcast_to(m + jnp.log2(l), (TQ_C, 2 * HEAD_DIM)))
            o_scr[pl.ds(q0, TQ_C), :] = jnp.where(lane_head == 0, o_h[0], o_h[1])
            lse_scr[pl.ds(q0, TQ_C), :] = jnp.where(lane_head == 0, lse_h[0], lse_h[1])
            return c2

        lax.fori_loop(0, nqb, q_body, 0)
        o_ref[0, rows, :] = o_scr[...]
        lse_ref[0, rows, :] = lse_scr[...]
        return carry

    lax.fori_loop(0, dil, sub_body, 0)


def _c_call(qkv, bias, g, dil):
    bt, n, _ = qkv.shape
    sub_len = n // dil
    win = min(2 * TQ_C, sub_len)
    lanes = 2 * HEAD_DIM
    return pl.pallas_call(
        functools.partial(_c_kernel, dil=dil),
        grid=(bt,),
        in_specs=[
            pl.BlockSpec((1, n, lanes), lambda b: (b, 0, g)),
            pl.BlockSpec((1, n, lanes), lambda b: (b, 0, 3 + g)),
            pl.BlockSpec((1, n, lanes), lambda b: (b, 0, 6 + g)),
            _const_spec((2, 3, TQ_C, win)),
        ],
        out_specs=[
            pl.BlockSpec((1, n, lanes), lambda b: (b, 0, 0)),
            pl.BlockSpec((1, n, lanes), lambda b: (b, 0, 0)),
        ],
        out_shape=[
            jax.ShapeDtypeStruct((bt, n, lanes), F32),
            jax.ShapeDtypeStruct((bt, n, lanes), F32),
        ],
        scratch_shapes=[
            pltpu.VMEM((3, sub_len, lanes), BF16),
            pltpu.VMEM((sub_len, lanes), F32),
            pltpu.VMEM((sub_len, lanes), F32),
        ],
        compiler_params=_cparams(("parallel",)),
        name=f"mixer_c{g}",
    )(qkv, qkv, qkv, bias)


def _out_kernel(x_ref, mod_ref, oa_ref, ob_ref, od_ref, oc0_ref, oc1_ref, oc2_ref,
                l0_ref, l1_ref, l2_ref, w_ref, o_ref):
    l0, l1, l2 = l0_ref[0], l1_ref[0], l2_ref[0]
    mx = jnp.maximum(jnp.maximum(l0, l1), l2)
    e0, e1, e2 = jnp.exp2(l0 - mx), jnp.exp2(l1 - mx), jnp.exp2(l2 - mx)
    inv = 1.0 / (e0 + e1 + e2)
    y = jnp.dot(oa_ref[0], w_ref[0:256, :], preferred_element_type=F32)
    y += jnp.dot(ob_ref[0], w_ref[256:512, :], preferred_element_type=F32)
    y += jnp.dot((oc0_ref[0] * (e0 * inv)).astype(BF16), w_ref[512:640, :], preferred_element_type=F32)
    y += jnp.dot((oc1_ref[0] * (e1 * inv)).astype(BF16), w_ref[640:768, :], preferred_element_type=F32)
    y += jnp.dot((oc2_ref[0] * (e2 * inv)).astype(BF16), w_ref[768:896, :], preferred_element_type=F32)
    y += jnp.dot(od_ref[0], w_ref[896:1152, :], preferred_element_type=F32)
    o_ref[0] = x_ref[0] + mod_ref[0, 2:3, :] * y


def _out_call(x, mod, o_a, o_b, o_d, o_c, lse_c, w_out):
    bt, n, d = x.shape
    tn = TN_OUT
    tok = lambda w: pl.BlockSpec((1, tn, w), lambda b, t: (b, t, 0))
    return pl.pallas_call(
        _out_kernel,
        grid=(bt, n // tn),
        in_specs=[tok(d), pl.BlockSpec((1, 6, d), lambda b, t: (b, 0, 0)),
                  tok(256), tok(256), tok(256), tok(128), tok(128), tok(128),
                  tok(128), tok(128), tok(128), _const_spec(w_out.shape)],
        out_specs=tok(d),
        out_shape=jax.ShapeDtypeStruct((bt, n, d), F32),
        compiler_params=_cparams(("parallel", "parallel")),
        name="out_proj",
    )(x, mod, o_a, o_b, o_d, *o_c, *lse_c, w_out)


def _ffn_kernel(xp_ref, x_ref, xn_ref, mod_ref, g_ref, wup_ref, cw_ref, wdn_ref, o_ref, h_scr, acc_scr):
    t = pl.program_id(1)
    nt = pl.num_programs(1)
    tn = x_ref.shape[1]
    ext = tn + 2 * HALO
    xe = jnp.concatenate([xp_ref[0], x_ref[0], xn_ref[0]], axis=0)
    h = _mod_norm(xe, g_ref[...], mod_ref[0, 4:5, :], mod_ref[0, 3:4, :])
    row = lax.broadcasted_iota(jnp.int32, (ext, 1), 0)
    keep = jnp.logical_and(jnp.logical_or(row >= HALO, t > 0),
                           jnp.logical_or(row < HALO + tn, t < nt - 1))
    h_scr[...] = jnp.where(keep, h, 0.0).astype(BF16)
    acc_scr[...] = jnp.zeros_like(acc_scr)

    def conv(u, cw):
        prev = pltpu.roll(u, 1, 0)[HALO:HALO + tn]
        nxt = pltpu.roll(u, ext - 1, 0)[HALO:HALO + tn]
        return prev * cw[0:1] + u[HALO:HALO + tn] * cw[1:2] + nxt * cw[2:3] + cw[3:4]

    def chunk_body(c, carry):
        hb = h_scr[...]
        val = conv(jnp.dot(hb, wup_ref[c], preferred_element_type=F32), cw_ref[c])
        gate = conv(jnp.dot(hb, wup_ref[N_FF_CHUNKS + c], preferred_element_type=F32),
                    cw_ref[N_FF_CHUNKS + c])
        act = (val * gate * (1.0 / (1.0 + jnp.exp(-gate)))).astype(BF16)
        acc_scr[...] += jnp.dot(act, wdn_ref[c], preferred_element_type=F32)
        return carry

    lax.fori_loop(0, N_FF_CHUNKS, chunk_body, 0)
    o_ref[0] = x_ref[0] + mod_ref[0, 5:6, :] * acc_scr[...]


def _ffn_call(x, mod, g, wup, cw, wdn):
    bt, n, d = x.shape
    tn = TN_FFN
    nb = tn // HALO
    nhalo = n // HALO
    return pl.pallas_call(
        _ffn_kernel,
        grid=(bt, n // tn),
        in_specs=[
            pl.BlockSpec((1, HALO, d), lambda b, t: (b, jnp.maximum(t * nb - 1, 0), 0)),
            pl.BlockSpec((1, tn, d), lambda b, t: (b, t, 0)),
            pl.BlockSpec((1, HALO, d), lambda b, t: (b, jnp.minimum((t + 1) * nb, nhalo - 1), 0)),
            pl.BlockSpec((1, 6, d), lambda b, t: (b, 0, 0)),
            _const_spec((1, d)),
            _const_spec(wup.shape),
            _const_spec(cw.shape),
            _const_spec(wdn.shape),
        ],
        out_specs=pl.BlockSpec((1, tn, d), lambda b, t: (b, t, 0)),
        out_shape=jax.ShapeDtypeStruct((bt, n, d), F32),
        scratch_shapes=[pltpu.VMEM((tn + 2 * HALO, d), BF16), pltpu.VMEM((tn, d), F32)],
        compiler_params=_cparams(("parallel", "parallel")),
        name="ffn",
    )(x, x, x, mod, g, wup, cw, wdn)


def _t5_bucket(rel):
    nb = T5_BUCKETS // 2
    max_exact = nb // 2
    base = jnp.where(rel > 0, nb, 0)
    n = jnp.abs(rel)
    nf = jnp.maximum(n, 1).astype(jnp.float32)
    large = max_exact + (jnp.log(nf / max_exact) / math.log(T5_MAX_DIST / max_exact)
                         * (nb - max_exact)).astype(jnp.int32)
    large = jnp.minimum(large, nb - 1)
    return base + jnp.where(n < max_exact, n, large)


def _t5_bias(table, rel, mask):
    b = table.astype(F32)[_t5_bucket(jnp.asarray(rel, dtype=jnp.int32))] * LOG2E
    b = jnp.where(jnp.asarray(mask)[..., None], b, NEG_INF)
    return jnp.moveaxis(b, -1, 0)


def _bias_tables_t5(t5_table):
    kk = np.arange(KB_D)[:, None]
    qq = np.arange(TQ_D)[None, :]
    far = 8 * T5_MAX_DIST
    rel_d = np.stack([np.full((KB_D, TQ_D), -far)] + [j * KB_D + kk - qq for j in (-1, 0, 1)]
                     + [np.full((KB_D, TQ_D), far)])
    tbl_d = _t5_bias(t5_table[:, T5_D0:T5_D0 + DIFF_HEADS], rel_d, np.ones_like(rel_d, bool))
    kk = np.arange(TQ_B)[:, None]
    qq = np.arange(TQ_B)[None, :]
    rel_b = np.stack([j * TQ_B + kk - qq for j in (-1, 0, 1)])
    tbl_b = _t5_bias(t5_table[:, T5_B0:T5_B0 + SW_HEADS], rel_b, np.abs(rel_b) <= SW_HALF_WINDOW)
    tbl_c = []
    for g, (w, r) in enumerate(DIL_PAIRS):
        sub_len = SEQ // r
        hw = w // (2 * r)
        win = min(2 * TQ_C, sub_len)
        qq = np.arange(TQ_C)[:, None]
        kk = np.arange(win)[None, :]
        offs = (0, -(TQ_C // 2), -(win - TQ_C))
        rel_c = np.stack([o + kk - qq for o in offs])
        tbl_c.append(_t5_bias(t5_table[:, T5_C0 + 2 * g:T5_C0 + 2 * g + 2], rel_c * r,
                              np.abs(rel_c) <= hw))
    return tbl_d, tbl_b, tbl_c


def _bias_table_a(rpb):
    rows = SEQ // GRID_W
    kh = min(NA_WIN_H, rows)
    wrows = A_WIN // GRID_W
    a = np.arange(2)[:, None, None, None]
    c = np.arange(GRID_W)[None, :, None, None]
    kr = np.arange(wrows)[None, None, :, None]
    kc = np.arange(GRID_W)[None, None, None, :]
    dr_l, dc_l, ok_l = [], [], []
    for r in (0, 2, 4, rows - 4, rows - 2):
        ws = min(int(np.clip(r - kh // 2, 0, rows - kh)), rows - wrows)
        qrow = r + a
        krow = ws + kr
        rsq = np.clip(qrow - kh // 2, 0, rows - kh)
        cs = np.clip(c - NA_WIN_W // 2, 0, GRID_W - NA_WIN_W)
        ok = (krow >= rsq) & (krow < rsq + kh) & (kc >= cs) & (kc < cs + NA_WIN_W)
        dr = np.clip(krow - qrow + NA_WIN_H - 1, 0, 2 * NA_WIN_H - 2)
        dc = np.clip(kc - c + NA_WIN_W - 1, 0, 2 * NA_WIN_W - 2)
        shape = (2, GRID_W, wrows, GRID_W)
        dr_l.append(np.broadcast_to(dr, shape).reshape(A_PAIR, A_WIN).T)
        dc_l.append(np.broadcast_to(dc, shape).reshape(A_PAIR, A_WIN).T)
        ok_l.append(np.broadcast_to(ok, shape).reshape(A_PAIR, A_WIN).T)
    dr, dc, ok = np.stack(dr_l), np.stack(dc_l), np.stack(ok_l)
    b = rpb.astype(F32)[:, dr, dc] * LOG2E
    return jnp.where(jnp.asarray(ok)[None], b, NEG_INF)


def _in_weights(w_in_l, qkn_a, qkn_b, qkn_c, qkn_d):
    a0, b0, c0, d0 = 0, 768, 1280, 2432
    cols = np.concatenate([
        np.arange(a0, a0 + 256), np.arange(b0, b0 + 256), np.arange(d0, d0 + 256),
        np.arange(d0 + 512, d0 + 768), np.arange(a0 + 512, a0 + 768), np.arange(b0 + 384, b0 + 512),
        np.arange(d0 + 256, d0 + 512), np.arange(a0 + 256, a0 + 512), np.arange(b0 + 256, b0 + 384),
        np.arange(c0, c0 + 768),
    ])
    wt = w_in_l[:, cols].T.astype(BF16)
    wvc = w_in_l[:, c0 + 768:c0 + 1152].astype(BF16)
    qs64 = HEAD_DIM ** -0.5 * LOG2E
    qs32 = DIFF_HALF ** -0.5 * LOG2E
    ones = jnp.ones((VT_ROWS,), F32)
    gs = jnp.concatenate([
        jnp.tile(qkn_a[0] * qs64, 4), jnp.tile(qkn_b[0] * qs64, 4), jnp.tile(qkn_d[0] * qs32, 8),
        ones,
        jnp.tile(qkn_d[1], 8), jnp.tile(qkn_a[1], 4), jnp.tile(qkn_b[1], 2),
        jnp.tile(qkn_c[0] * qs64, 6), jnp.tile(qkn_c[1], 6),
    ]).astype(F32).reshape(WT_ROWS, 1)
    return wt, wvc, gs


def _ffn_weights(w_up_l, conv_w_l, conv_b_l, w_down_l):
    d = w_up_l.shape[0]
    wup = w_up_l.reshape(d, 2 * N_FF_CHUNKS, FF_CHUNK).transpose(1, 0, 2).astype(BF16)
    cw = jnp.concatenate([conv_w_l, conv_b_l[None], jnp.zeros((4, 2 * D_FF), F32)], axis=0)
    cw = cw.reshape(8, 2 * N_FF_CHUNKS, FF_CHUNK).transpose(1, 0, 2)
    wdn = w_down_l.reshape(N_FF_CHUNKS, FF_CHUNK, d).astype(BF16)
    return wup, cw, wdn


def _layer(x, mod, l, p, tables):
    tbl_d, tbl_b, tbl_c = tables
    wt, wvc, gs = _in_weights(p["w_in"][l], p["qkn_a"][l], p["qkn_b"][l], p["qkn_c"][l], p["qkn_d"][l])
    qT, vT, k, qkv_c = _in_call(x, mod, p["norm_attn_g"][l].reshape(1, -1), wt, wvc, gs)
    lambda_init = 0.8 - 0.6 * math.exp(-0.3 * l)
    o_a = _a_call(qT, k, vT, _bias_table_a(p["rpb_a"][l]))
    o_b = _b_call(p["sink_b"][l], qT, k, vT, tbl_b)
    o_c, lse_c = [], []
    for g, (_, r) in enumerate(DIL_PAIRS):
        o, lse = _c_call(qkv_c, tbl_c[g], g, r)
        o_c.append(o)
        lse_c.append(lse)
    o_d = _d_call(p["lam_d"][l], p["subln_d"][l].reshape(HEAD_DIM, 1), qT, k, vT, tbl_d, lambda_init)
    x = _out_call(x, mod, o_a, o_b, o_d, o_c, lse_c, p["w_out"][l].astype(BF16))
    wup, cw, wdn = _ffn_weights(p["w_up"][l], p["conv_w"][l], p["conv_b"][l], p["w_down"][l])
    return _ffn_call(x, mod, p["norm_ffn_g"][l].reshape(1, -1), wup, cw, wdn)


def _trunk(x, c, p, depth=DEPTH):
    bt, _, d = x.shape
    mod_all = _ada_call(c, p["w_ada"], p["b_ada"]).reshape(DEPTH, bt, 6, d)
    tables = _bias_tables_t5(p["t5_table"])
    for l in range(depth):
        x = _layer(x, mod_all[l], l, p, tables)
    return x


def kernel(x_prompt, x_sample, c_prompt, c_sample, norm_attn_g, norm_ffn_g, w_ada, b_ada, w_in, qkn_a, qkn_b, qkn_c, qkn_d, rpb_a, sink_b, t5_table, lam_d, subln_d, w_out, w_up, conv_w, conv_b, w_down):
    p = dict(norm_attn_g=norm_attn_g, norm_ffn_g=norm_ffn_g, w_ada=w_ada, b_ada=b_ada, w_in=w_in,
             qkn_a=qkn_a, qkn_b=qkn_b, qkn_c=qkn_c, qkn_d=qkn_d, rpb_a=rpb_a, sink_b=sink_b,
             t5_table=t5_table, lam_d=lam_d, subln_d=subln_d, w_out=w_out, w_up=w_up,
             conv_w=conv_w, conv_b=conv_b, w_down=w_down)
    nb = x_prompt.shape[0]
    x = jnp.concatenate([x_prompt, x_sample], axis=0)
    c = jnp.concatenate([c_prompt, c_sample], axis=0)
    y = _trunk(x, c, p)
    return (y[:nb], y[nb:])
```

```python
import functools
import math

import numpy as np
import jax
import jax.numpy as jnp
from jax import lax
from jax.experimental import pallas as pl
from jax.experimental.pallas import tpu as pltpu

D_MODEL = 1024
SEQ = 2048
DEPTH = 4
HEAD_DIM = 64
GRID_W = 64
NA_HEADS = 4
NA_WIN_H = 8
NA_WIN_W = 16
SW_HEADS = 4
SW_KV_HEADS = 2
SW_HALF_WINDOW = 128
DIL_PAIRS = ((128, 1), (512, 4), (2048, 16))
DIFF_HEADS = 4
DIFF_HALF = HEAD_DIM // 2
T5_BUCKETS = 32
T5_MAX_DIST = 128
T5_B0 = 0
T5_C0 = 4
T5_D0 = 10
D_FF = 2816
RMS_EPS = 1e-6
NEG_INF = -1e30
LOG2E = 1.4426950408889634

F32 = jnp.float32
BF16 = jnp.bfloat16

VMEM_LIMIT_BYTES = 56 * 1024 * 1024
TN_IN = 512
TN_OUT = 512
TN_FFN = 512
FF_CHUNK = 256
N_FF_CHUNKS = D_FF // FF_CHUNK
HALO = 8
TQ_D = 256
KB_D = 256
TQ_B = 128
TQ_C = 128
A_PAIR = 2 * GRID_W
A_WIN = 10 * GRID_W

QT_ROWS = 768
VT_ROWS = 640
K_ROWS = 640
C_QK_ROWS = 768
WT_ROWS = QT_ROWS + VT_ROWS + K_ROWS + C_QK_ROWS
C_WIDTH = 1152


def _cparams(sem):
    return pltpu.CompilerParams(dimension_semantics=sem, vmem_limit_bytes=VMEM_LIMIT_BYTES)


def _const_spec(shape):
    nd = len(shape)
    return pl.BlockSpec(shape, lambda *_: (0,) * nd, pipeline_mode=pl.Buffered(1))


def _ada_kernel(c_ref, w_ref, b_ref, o_ref):
    c = c_ref[...]
    sc = (c * (1.0 / (1.0 + jnp.exp(-c)))).astype(BF16)
    o_ref[0] = jnp.dot(sc, w_ref[0].astype(BF16), preferred_element_type=F32) + b_ref[0]


def _ada_call(c, w_ada, b_ada):
    bt = c.shape[0]
    ncol = 6 * D_MODEL
    cb = 1024
    return pl.pallas_call(
        _ada_kernel,
        grid=(DEPTH, ncol // cb),
        in_specs=[
            pl.BlockSpec((bt, D_MODEL), lambda l, j: (0, 0)),
            pl.BlockSpec((1, D_MODEL, cb), lambda l, j: (l, 0, j)),
            pl.BlockSpec((1, 1, cb), lambda l, j: (l, 0, j)),
        ],
        out_specs=pl.BlockSpec((1, bt, cb), lambda l, j: (l, 0, j)),
        out_shape=jax.ShapeDtypeStruct((DEPTH, bt, ncol), F32),
        compiler_params=_cparams(("arbitrary", "arbitrary")),
        name="ada_mod",
    )(c, w_ada, b_ada.reshape(DEPTH, 1, ncol))


def _mod_norm(x, g, scale, shift):
    ms = jnp.mean(x * x, axis=-1, keepdims=True)
    return x * lax.rsqrt(ms + RMS_EPS) * (g * (1.0 + scale)) + shift


def _in_kernel(x_ref, mod_ref, g_ref, wt_ref, wvc_ref, gs_ref, qT_ref, vT_ref, k_ref, c_ref, h_scr):
    tn = x_ref.shape[1]
    h = _mod_norm(x_ref[0], g_ref[...], mod_ref[0, 1:2, :], mod_ref[0, 0:1, :])
    h_scr[...] = h.astype(BF16)
    hb = h_scr[...]

    def proj_t(r0, rows):
        return lax.dot_general(wt_ref[r0:r0 + rows, :], hb, (((1,), (1,)), ((), ())),
                               preferred_element_type=F32)

    def head_norm(y, r0, grp):
        rows = y.shape[0]
        y3 = y.reshape(rows // grp, grp, tn)
        ms = jnp.sum(y3 * y3, axis=1, keepdims=True) * (1.0 / grp)
        gs = gs_ref[r0:r0 + rows, :].reshape(rows // grp, grp, 1)
        return (y3 * lax.rsqrt(ms + RMS_EPS) * gs).reshape(rows, tn)

    for r0, grp in ((0, HEAD_DIM), (256, HEAD_DIM), (512, DIFF_HALF)):
        qT_ref[0, r0:r0 + 256, :] = head_norm(proj_t(r0, 256), r0, grp).astype(BF16)
    base = QT_ROWS
    for r0, rows in ((0, 256), (256, 256), (512, 128)):
        vT_ref[0, r0:r0 + rows, :] = proj_t(base + r0, rows).astype(BF16)
    base = QT_ROWS + VT_ROWS
    for r0, rows, grp in ((0, 256, DIFF_HALF), (256, 256, HEAD_DIM), (512, 128, HEAD_DIM)):
        kt = head_norm(proj_t(base + r0, rows), base + r0, grp)
        k_ref[0, :, r0:r0 + rows] = kt.T.astype(BF16)
    base = QT_ROWS + VT_ROWS + K_ROWS
    for r0 in (0, 256, 512):
        ct = head_norm(proj_t(base + r0, 256), base + r0, HEAD_DIM)
        c_ref[0, :, r0:r0 + 256] = ct.T
    c_ref[0, :, C_QK_ROWS:C_WIDTH] = jnp.dot(hb, wvc_ref[...], preferred_element_type=F32)


def _in_call(x, mod, g, wt, wvc, gs):
    bt, n, d = x.shape
    tn = TN_IN
    return pl.pallas_call(
        _in_kernel,
        grid=(bt, n // tn),
        in_specs=[
            pl.BlockSpec((1, tn, d), lambda b, t: (b, t, 0)),
            pl.BlockSpec((1, 6, d), lambda b, t: (b, 0, 0)),
            _const_spec((1, d)),
            _const_spec((WT_ROWS, d)),
            _const_spec((d, C_WIDTH - C_QK_ROWS)),
            _const_spec((WT_ROWS, 1)),
        ],
        out_specs=[
            pl.BlockSpec((1, QT_ROWS, tn), lambda b, t: (b, 0, t)),
            pl.BlockSpec((1, VT_ROWS, tn), lambda b, t: (b, 0, t)),
            pl.BlockSpec((1, tn, K_ROWS), lambda b, t: (b, t, 0)),
            pl.BlockSpec((1, tn, C_WIDTH), lambda b, t: (b, t, 0)),
        ],
        out_shape=[
            jax.ShapeDtypeStruct((bt, QT_ROWS, n), BF16),
            jax.ShapeDtypeStruct((bt, VT_ROWS, n), BF16),
            jax.ShapeDtypeStruct((bt, n, K_ROWS), BF16),
            jax.ShapeDtypeStruct((bt, n, C_WIDTH), F32),
        ],
        scratch_shapes=[pltpu.VMEM((tn, d), BF16)],
        compiler_params=_cparams(("parallel", "parallel")),
        name="in_proj",
    )(x, mod, g, wt, wvc, gs)


def _d_kernel(lam_ref, sub_ref, q_ref, k_ref, v_ref, tbl_ref, o_ref, s_scr, p_scr, *, lambda_init):
    t = pl.program_id(1)
    tq = q_ref.shape[2]
    nkb = k_ref.shape[1] // KB_D
    qT = q_ref[0]
    rowgrp = lax.broadcasted_iota(jnp.int32, qT.shape, 0) // DIFF_HALF
    lp = lam_ref[...]
    lam = (jnp.exp(jnp.sum(lp[0:1] * lp[1:2], axis=-1, keepdims=True))
           - jnp.exp(jnp.sum(lp[2:3] * lp[3:4], axis=-1, keepdims=True)) + lambda_init)

    blocks = []
    for d in range(-1, nkb - 1):
        jb = lax.rem(t + d + nkb, nkb)
        blocks.append((d <= 1, pl.multiple_of(jb * KB_D, KB_D), jnp.clip(jb - t, -2, 2) + 2))

    def softmax_pv(g):
        h, slot = g // 2, g % 2
        qg = jnp.where(rowgrp == g, qT, jnp.zeros_like(qT))
        m = jnp.full((1, tq), -jnp.inf, F32)
        consts = []
        for pos, (near, k0, idx) in enumerate(blocks):
            s = jnp.dot(k_ref[0, pl.ds(k0, KB_D), :], qg, preferred_element_type=F32)
            if near:
                s = s + tbl_ref[h, idx]
                cvec = None
            else:
                cvec = tbl_ref[h, idx, 0:1, :]
            s_scr[slot, pos] = s
            bm = jnp.max(jnp.max(s.reshape(KB_D // 8, 8, tq), axis=0), axis=0, keepdims=True)
            m = jnp.maximum(m, bm if near else bm + cvec)
            consts.append(cvec)
        l8 = jnp.zeros((8, tq), F32)
        for pos, (near, k0, idx) in enumerate(blocks):
            p = jnp.exp2(s_scr[slot, pos] - (m if near else m - consts[pos]))
            l8 = l8 + jnp.sum(p.reshape(KB_D // 8, 8, tq), axis=0)
            p_scr[slot, pl.ds(k0, KB_D), :] = p.astype(BF16)
        l = jnp.sum(l8, axis=0, keepdims=True)
        vh = v_ref[0, h * HEAD_DIM:(h + 1) * HEAD_DIM, :]
        return jnp.dot(vh, p_scr[slot], preferred_element_type=F32) * (1.0 / l)

    parts = []
    for h in range(DIFF_HEADS):
        oh = softmax_pv(2 * h) - lam * softmax_pv(2 * h + 1)
        ms = jnp.mean(oh * oh, axis=0, keepdims=True)
        parts.append(oh * lax.rsqrt(ms + RMS_EPS) * sub_ref[...] * (1.0 - lambda_init))
    o_ref[0] = jnp.concatenate(parts, axis=0).T.astype(BF16)


def _d_call(lam_p, subln, qT, k, vT, tbl, lambda_init):
    bt, _, n = qT.shape
    tq = TQ_D
    return pl.pallas_call(
        functools.partial(_d_kernel, lambda_init=lambda_init),
        grid=(bt, n // tq),
        in_specs=[
            _const_spec((4, DIFF_HALF)),
            _const_spec((HEAD_DIM, 1)),
            pl.BlockSpec((1, 256, tq), lambda b, t: (b, 2, t)),
            pl.BlockSpec((1, n, 256), lambda b, t: (b, 0, 0)),
            pl.BlockSpec((1, 256, n), lambda b, t: (b, 0, 0)),
            _const_spec((DIFF_HEADS, 5, KB_D, tq)),
        ],
        out_specs=pl.BlockSpec((1, tq, 256), lambda b, t: (b, t, 0)),
        out_shape=jax.ShapeDtypeStruct((bt, n, 256), BF16),
        scratch_shapes=[
            pltpu.VMEM((2, n // KB_D, KB_D, tq), F32),
            pltpu.VMEM((2, n, tq), BF16),
        ],
        compiler_params=_cparams(("parallel", "arbitrary")),
        name="mixer_d",
    )(lam_p, subln, qT, k, vT, tbl)


def _b_kernel(sink_ref, q_ref, k_ref, v_ref, bias_ref, o_ref):
    n = k_ref.shape[1]
    tq = TQ_B
    nt = n // tq
    grp = SW_HEADS // SW_KV_HEADS

    def tile_body(t, carry):
        q0 = pl.multiple_of(t * tq, tq)
        qT = q_ref[0, :, pl.ds(q0, tq)]
        zero = jnp.zeros((HEAD_DIM, tq), BF16)
        outs = []
        for h in range(SW_HEADS):
            kv = h // grp
            qh = qT[h * HEAD_DIM:(h + 1) * HEAD_DIM]
            qsel = jnp.concatenate([qh, zero] if kv == 0 else [zero, qh], axis=0)
            sink = sink_ref[h] * LOG2E
            ss, starts = [], []
            m = jnp.full((1, tq), sink, F32)
            for j in (-1, 0, 1):
                jc = jnp.clip(t + j, 0, nt - 1)
                k0 = pl.multiple_of(jc * tq, tq)
                valid = jnp.logical_and(t + j >= 0, t + j <= nt - 1)
                s = jnp.dot(k_ref[0, pl.ds(k0, tq), :], qsel, preferred_element_type=F32)
                s = s + jnp.where(valid, bias_ref[h, j + 1], NEG_INF)
                m = jnp.maximum(m, jnp.max(s, axis=0, keepdims=True))
                ss.append(s)
                starts.append(k0)
            l = jnp.exp2(sink - m)
            o = jnp.zeros((HEAD_DIM, tq), F32)
            for s, k0 in zip(ss, starts):
                p = jnp.exp2(s - m)
                l = l + jnp.sum(p, axis=0, keepdims=True)
                vh = v_ref[0, kv * HEAD_DIM:(kv + 1) * HEAD_DIM, pl.ds(k0, tq)]
                o = o + jnp.dot(vh, p.astype(BF16), preferred_element_type=F32)
            outs.append(o * (1.0 / l))
        o_ref[0, pl.ds(q0, tq), :] = jnp.concatenate(outs, axis=0).T.astype(BF16)
        return carry

    lax.fori_loop(0, nt, tile_body, 0)


def _b_call(sink, qT, k, vT, bias):
    bt, _, n = qT.shape
    return pl.pallas_call(
        _b_kernel,
        grid=(bt,),
        in_specs=[
            pl.BlockSpec(memory_space=pltpu.SMEM),
            pl.BlockSpec((1, 256, n), lambda b: (b, 1, 0)),
            pl.BlockSpec((1, n, 128), lambda b: (b, 0, 4)),
            pl.BlockSpec((1, 128, n), lambda b: (b, 4, 0)),
            _const_spec((SW_HEADS, 3, TQ_B, TQ_B)),
        ],
        out_specs=pl.BlockSpec((1, n, 256), lambda b: (b, 0, 0)),
        out_shape=jax.ShapeDtypeStruct((bt, n, 256), BF16),
        compiler_params=_cparams(("parallel",)),
        name="mixer_b",
    )(sink, qT, k, vT, bias)


def _a_kernel(q_ref, k_ref, v_ref, bias_ref, o_ref):
    n = k_ref.shape[1]
    rows = n // GRID_W
    npair = n // A_PAIR
    kh = min(NA_WIN_H, rows)

    def pair_body(rp, carry):
        r = 2 * rp
        rs = jnp.clip(r - kh // 2, 0, rows - kh)
        ws = jnp.minimum(rs, rows - A_WIN // GRID_W)
        var = jnp.where(rp < 2, rp, jnp.where(rp < npair - 2, 2, rp - (npair - 5)))
        q0 = pl.multiple_of(rp * A_PAIR, A_PAIR)
        k0 = pl.multiple_of(ws * GRID_W, A_PAIR)
        qT = q_ref[0, :, pl.ds(q0, A_PAIR)]
        kw = k_ref[0, pl.ds(k0, A_WIN), :]
        rowhead = lax.broadcasted_iota(jnp.int32, qT.shape, 0) // HEAD_DIM
        outs = []
        for h in range(NA_HEADS):
            qsel = jnp.where(rowhead == h, qT, jnp.zeros_like(qT))
            s = jnp.dot(kw, qsel, preferred_element_type=F32) + bias_ref[h, var]
            m = jnp.max(s, axis=0, keepdims=True)
            p = jnp.exp2(s - m)
            l = jnp.sum(p, axis=0, keepdims=True)
            vh = v_ref[0, h * HEAD_DIM:(h + 1) * HEAD_DIM, pl.ds(k0, A_WIN)]
            outs.append(jnp.dot(vh, p.astype(BF16), preferred_element_type=F32) * (1.0 / l))
        o_ref[0, pl.ds(q0, A_PAIR), :] = jnp.concatenate(outs, axis=0).T.astype(BF16)
        return carry

    lax.fori_loop(0, npair, pair_body, 0)


def _a_call(qT, k, vT, bias, layer):
    bt, _, n = qT.shape
    return pl.pallas_call(
        _a_kernel,
        grid=(bt,),
        in_specs=[
            pl.BlockSpec((1, 256, n), lambda b: (b, 0, 0)),
            pl.BlockSpec((1, n, 256), lambda b: (b, 0, 1)),
            pl.BlockSpec((1, 256, n), lambda b: (b, 1, 0)),
            pl.BlockSpec((NA_HEADS, 5, A_WIN, A_PAIR), lambda b: (layer, 0, 0, 0),
                         pipeline_mode=pl.Buffered(1)),
        ],
        out_specs=pl.BlockSpec((1, n, 256), lambda b: (b, 0, 0)),
        out_shape=jax.ShapeDtypeStruct((bt, n, 256), BF16),
        compiler_params=_cparams(("parallel",)),
        name="mixer_a",
    )(qT, k, vT, bias)


def _c_kernel(q_ref, k_ref, v_ref, bias_ref, o_ref, lse_ref, sub_scr, o_scr, lse_scr, *, dil):
    n = q_ref.shape[1]
    sub_len = n // dil
    win = min(2 * TQ_C, sub_len)
    nqb = sub_len // TQ_C
    lane_head = lax.broadcasted_iota(jnp.int32, (TQ_C, 2 * HEAD_DIM), 1) // HEAD_DIM

    def sub_body(p, carry):
        rows = pl.ds(p, sub_len, stride=dil) if dil > 1 else pl.ds(0, sub_len)
        sub_scr[0] = q_ref[0, rows, :].astype(BF16)
        sub_scr[1] = k_ref[0, rows, :].astype(BF16)
        sub_scr[2] = v_ref[0, rows, :].astype(BF16)

        def q_body(qb, c2):
            q0 = pl.multiple_of(qb * TQ_C, TQ_C)
            start = pl.multiple_of(jnp.clip(q0 - TQ_C // 2, 0, sub_len - win), TQ_C // 2)
            var = jnp.where(qb == 0, 0, jnp.where(qb == nqb - 1, 2, 1))
            qblk = sub_scr[0, pl.ds(q0, TQ_C), :]
            kw = sub_scr[1, pl.ds(start, win), :]
            vw = sub_scr[2, pl.ds(start, win), :]
            o_h, lse_h = [], []
            for j in range(2):
                qm = jnp.where(lane_head == j, qblk, jnp.zeros_like(qblk))
                s = lax.dot_general(qm, kw, (((1,), (1,)), ((), ())), preferred_element_type=F32)
                s = s + bias_ref[j, var]
                m = jnp.max(s, axis=-1, keepdims=True)
                pr = jnp.exp2(s - m)
                l = jnp.sum(pr, axis=-1, keepdims=True)
                o_h.append(jnp.dot(pr.astype(BF16), vw, preferred_element_type=F32) * (1.0 / l))
                lse_h.append(jnp.broadcast_to(m + jnp.log2(l), (TQ_C, 2 * HEAD_DIM)))
            o_scr[pl.ds(q0, TQ_C), :] = jnp.where(lane_head == 0, o_h[0], o_h[1])
            lse_scr[pl.ds(q0, TQ_C), :] = jnp.where(lane_head == 0, lse_h[0], lse_h[1])
            return c2

        lax.fori_loop(0, nqb, q_body, 0)
        o_ref[0, rows, :] = o_scr[...]
        lse_ref[0, rows, :] = lse_scr[...]
        return carry

    lax.fori_loop(0, dil, sub_body, 0)


def _c_call(qkv, bias, g, dil):
    bt, n, _ = qkv.shape
    sub_len = n // dil
    win = min(2 * TQ_C, sub_len)
    lanes = 2 * HEAD_DIM
    return pl.pallas_call(
        functools.partial(_c_kernel, dil=dil),
        grid=(bt,),
        in_specs=[
            pl.BlockSpec((1, n, lanes), lambda b: (b, 0, g)),
            pl.BlockSpec((1, n, lanes), lambda b: (b, 0, 3 + g)),
            pl.BlockSpec((1, n, lanes), lambda b: (b, 0, 6 + g)),
            _const_spec((2, 3, TQ_C, win)),
        ],
        out_specs=[
            pl.BlockSpec((1, n, lanes), lambda b: (b, 0, 0)),
            pl.BlockSpec((1, n, lanes), lambda b: (b, 0, 0)),
        ],
        out_shape=[
            jax.ShapeDtypeStruct((bt, n, lanes), F32),
            jax.ShapeDtypeStruct((bt, n, lanes), F32),
        ],
        scratch_shapes=[
            pltpu.VMEM((3, sub_len, lanes), BF16),
            pltpu.VMEM((sub_len, lanes), F32),
            pltpu.VMEM((sub_len, lanes), F32),
        ],
        compiler_params=_cparams(("parallel",)),
        name=f"mixer_c{g}",
    )(qkv, qkv, qkv, bias)


def _out_kernel(x_ref, mod_ref, oa_ref, ob_ref, od_ref, oc0_ref, oc1_ref, oc2_ref,
                l0_ref, l1_ref, l2_ref, w_ref, o_ref):
    l0, l1, l2 = l0_ref[0], l1_ref[0], l2_ref[0]
    mx = jnp.maximum(jnp.maximum(l0, l1), l2)
    e0, e1, e2 = jnp.exp2(l0 - mx), jnp.exp2(l1 - mx), jnp.exp2(l2 - mx)
    inv = 1.0 / (e0 + e1 + e2)
    y = jnp.dot(oa_ref[0], w_ref[0:256, :], preferred_element_type=F32)
    y += jnp.dot(ob_ref[0], w_ref[256:512, :], preferred_element_type=F32)
    y += jnp.dot((oc0_ref[0] * (e0 * inv)).astype(BF16), w_ref[512:640, :], preferred_element_type=F32)
    y += jnp.dot((oc1_ref[0] * (e1 * inv)).astype(BF16), w_ref[640:768, :], preferred_element_type=F32)
    y += jnp.dot((oc2_ref[0] * (e2 * inv)).astype(BF16), w_ref[768:896, :], preferred_element_type=F32)
    y += jnp.dot(od_ref[0], w_ref[896:1152, :], preferred_element_type=F32)
    o_ref[0] = x_ref[0] + mod_ref[0, 2:3, :] * y


def _out_call(x, mod, o_a, o_b, o_d, o_c, lse_c, w_out):
    bt, n, d = x.shape
    tn = TN_OUT
    tok = lambda w: pl.BlockSpec((1, tn, w), lambda b, t: (b, t, 0))
    return pl.pallas_call(
        _out_kernel,
        grid=(bt, n // tn),
        in_specs=[tok(d), pl.BlockSpec((1, 6, d), lambda b, t: (b, 0, 0)),
                  tok(256), tok(256), tok(256), tok(128), tok(128), tok(128),
                  tok(128), tok(128), tok(128), _const_spec(w_out.shape)],
        out_specs=tok(d),
        out_shape=jax.ShapeDtypeStruct((bt, n, d), F32),
        compiler_params=_cparams(("parallel", "parallel")),
        name="out_proj",
    )(x, mod, o_a, o_b, o_d, *o_c, *lse_c, w_out)


def _ffn_kernel(xp_ref, x_ref, xn_ref, mod_ref, g_ref, wup_ref, cw_ref, wdn_ref, o_ref, h_scr, acc_scr):
    t = pl.program_id(1)
    nt = pl.num_programs(1)
    tn = x_ref.shape[1]
    ext = tn + 2 * HALO
    xe = jnp.concatenate([xp_ref[0], x_ref[0], xn_ref[0]], axis=0)
    h = _mod_norm(xe, g_ref[...], mod_ref[0, 4:5, :], mod_ref[0, 3:4, :])
    row = lax.broadcasted_iota(jnp.int32, (ext, 1), 0)
    keep = jnp.logical_and(jnp.logical_or(row >= HALO, t > 0),
                           jnp.logical_or(row < HALO + tn, t < nt - 1))
    h_scr[...] = jnp.where(keep, h, 0.0).astype(BF16)
    acc_scr[...] = jnp.zeros_like(acc_scr)

    def conv(u, cw):
        prev = pltpu.roll(u, 1, 0)[HALO:HALO + tn]
        nxt = pltpu.roll(u, ext - 1, 0)[HALO:HALO + tn]
        return prev * cw[0:1] + u[HALO:HALO + tn] * cw[1:2] + nxt * cw[2:3] + cw[3:4]

    hb = h_scr[...]
    for c in range(N_FF_CHUNKS):
        val = conv(jnp.dot(hb, wup_ref[c], preferred_element_type=F32), cw_ref[c])
        gate = conv(jnp.dot(hb, wup_ref[N_FF_CHUNKS + c], preferred_element_type=F32),
                    cw_ref[N_FF_CHUNKS + c])
        act = (val * gate * (1.0 / (1.0 + jnp.exp(-gate)))).astype(BF16)
        acc_scr[...] += jnp.dot(act, wdn_ref[c], preferred_element_type=F32)
    o_ref[0] = x_ref[0] + mod_ref[0, 5:6, :] * acc_scr[...]


def _ffn_call(x, mod, g, wup, cw, wdn):
    bt, n, d = x.shape
    tn = TN_FFN
    nb = tn // HALO
    nhalo = n // HALO
    return pl.pallas_call(
        _ffn_kernel,
        grid=(bt, n // tn),
        in_specs=[
            pl.BlockSpec((1, HALO, d), lambda b, t: (b, jnp.maximum(t * nb - 1, 0), 0)),
            pl.BlockSpec((1, tn, d), lambda b, t: (b, t, 0)),
            pl.BlockSpec((1, HALO, d), lambda b, t: (b, jnp.minimum((t + 1) * nb, nhalo - 1), 0)),
            pl.BlockSpec((1, 6, d), lambda b, t: (b, 0, 0)),
            _const_spec((1, d)),
            _const_spec(wup.shape),
            _const_spec(cw.shape),
            _const_spec(wdn.shape),
        ],
        out_specs=pl.BlockSpec((1, tn, d), lambda b, t: (b, t, 0)),
        out_shape=jax.ShapeDtypeStruct((bt, n, d), F32),
        scratch_shapes=[pltpu.VMEM((tn + 2 * HALO, d), BF16), pltpu.VMEM((tn, d), F32)],
        compiler_params=_cparams(("parallel", "parallel")),
        name="ffn",
    )(x, x, x, mod, g, wup, cw, wdn)


def _t5_bucket(rel):
    nb = T5_BUCKETS // 2
    max_exact = nb // 2
    base = jnp.where(rel > 0, nb, 0)
    n = jnp.abs(rel)
    nf = jnp.maximum(n, 1).astype(jnp.float32)
    large = max_exact + (jnp.log(nf / max_exact) / math.log(T5_MAX_DIST / max_exact)
                         * (nb - max_exact)).astype(jnp.int32)
    large = jnp.minimum(large, nb - 1)
    return base + jnp.where(n < max_exact, n, large)


def _t5_kernel(tab_ref, bkt_ref, o_ref, *, head0):
    h = head0 + pl.program_id(0)
    for v in range(bkt_ref.shape[0]):
        bkt = bkt_ref[v]
        acc = jnp.full(bkt.shape, NEG_INF, F32)
        for b in range(T5_BUCKETS):
            acc = jnp.where(bkt == b, tab_ref[b, h] * LOG2E, acc)
        o_ref[0, v] = acc


def _t5_bias(table, rel, mask, head0, nheads):
    bkt = jnp.where(jnp.asarray(mask), _t5_bucket(jnp.asarray(rel, dtype=jnp.int32)), -1)
    return pl.pallas_call(
        functools.partial(_t5_kernel, head0=head0),
        grid=(nheads,),
        in_specs=[pl.BlockSpec(memory_space=pltpu.SMEM), _const_spec(bkt.shape)],
        out_specs=pl.BlockSpec((1,) + bkt.shape, lambda h: (h, 0, 0, 0)),
        out_shape=jax.ShapeDtypeStruct((nheads,) + bkt.shape, F32),
        compiler_params=_cparams(("arbitrary",)),
        name="t5_bias",
    )(table.astype(F32), bkt)


def _bias_tables_t5(t5_table):
    kk = np.arange(KB_D)[:, None]
    qq = np.arange(TQ_D)[None, :]
    far = 8 * T5_MAX_DIST
    rel_d = np.stack([np.full((KB_D, TQ_D), -far)] + [j * KB_D + kk - qq for j in (-1, 0, 1)]
                     + [np.full((KB_D, TQ_D), far)])
    tbl_d = _t5_bias(t5_table, rel_d, np.ones_like(rel_d, bool), T5_D0, DIFF_HEADS)
    kk = np.arange(TQ_B)[:, None]
    qq = np.arange(TQ_B)[None, :]
    rel_b = np.stack([j * TQ_B + kk - qq for j in (-1, 0, 1)])
    tbl_b = _t5_bias(t5_table, rel_b, np.abs(rel_b) <= SW_HALF_WINDOW, T5_B0, SW_HEADS)
    tbl_c = []
    for g, (w, r) in enumerate(DIL_PAIRS):
        sub_len = SEQ // r
        hw = w // (2 * r)
        win = min(2 * TQ_C, sub_len)
        qq = np.arange(TQ_C)[:, None]
        kk = np.arange(win)[None, :]
        offs = (0, -(TQ_C // 2), -(win - TQ_C))
        rel_c = np.stack([o + kk - qq for o in offs])
        tbl_c.append(_t5_bias(t5_table, rel_c * r, np.abs(rel_c) <= hw, T5_C0 + 2 * g, 2))
    return tbl_d, tbl_b, tbl_c


def _a_variants():
    rows = SEQ // GRID_W
    kh = min(NA_WIN_H, rows)
    wrows = A_WIN // GRID_W
    out = []
    for r in (0, 2, 4, rows - 4, rows - 2):
        ws = min(int(np.clip(r - kh // 2, 0, rows - kh)), rows - wrows)
        per_row = []
        for kr in range(wrows):
            pair = []
            for a in range(2):
                rsq = int(np.clip(r + a - kh // 2, 0, rows - kh))
                krow = ws + kr
                pair.append(krow - (r + a) + NA_WIN_H - 1 if rsq <= krow < rsq + kh else None)
            per_row.append(tuple(pair))
        out.append(tuple(per_row))
    return tuple(out)


def _abias_kernel(rpb_ref, dcm_ref, o_ref):
    i = pl.program_id(0)
    ncol = 2 * NA_WIN_W - 1
    dcm = dcm_ref[...]
    second = lax.broadcasted_iota(jnp.int32, dcm.shape, 1) >= GRID_W
    neg = jnp.full(dcm.shape, NEG_INF, F32)
    for v, per_row in enumerate(_a_variants()):
        for kr, (dr0, dr1) in enumerate(per_row):
            if dr0 is None and dr1 is None:
                o_ref[0, v, kr * GRID_W:(kr + 1) * GRID_W, :] = neg
                continue

            def body(t, acc, dr0=dr0, dr1=dr1):
                s0 = NEG_INF if dr0 is None else rpb_ref[i, dr0 * ncol + t] * LOG2E
                s1 = NEG_INF if dr1 is None else rpb_ref[i, dr1 * ncol + t] * LOG2E
                return jnp.where(dcm == t, jnp.where(second, s1, s0), acc)

            o_ref[0, v, kr * GRID_W:(kr + 1) * GRID_W, :] = lax.fori_loop(0, ncol, body, neg)


def _bias_tables_a(rpb_a):
    nl, nh, nr, nc = rpb_a.shape
    kc = np.arange(GRID_W)[:, None]
    c = np.tile(np.arange(GRID_W), 2)[None, :]
    cs = np.clip(c - NA_WIN_W // 2, 0, GRID_W - NA_WIN_W)
    dcm = np.where((kc >= cs) & (kc < cs + NA_WIN_W), np.clip(kc - c + NA_WIN_W - 1, 0, nc - 1), -1)
    return pl.pallas_call(
        _abias_kernel,
        grid=(nl * nh,),
        in_specs=[pl.BlockSpec(memory_space=pltpu.SMEM), _const_spec(dcm.shape)],
        out_specs=pl.BlockSpec((1, 5, A_WIN, A_PAIR), lambda i: (i, 0, 0, 0)),
        out_shape=jax.ShapeDtypeStruct((nl * nh, 5, A_WIN, A_PAIR), F32),
        compiler_params=_cparams(("arbitrary",)),
        name="rpb_bias",
    )(rpb_a.reshape(nl * nh, nr * nc).astype(F32), jnp.asarray(dcm, jnp.int32))


def _in_weights(w_in_l, qkn_a, qkn_b, qkn_c, qkn_d):
    a0, b0, c0, d0 = 0, 768, 1280, 2432
    segs = ((a0, 256), (b0, 256), (d0, 256),
            (d0 + 512, 256), (a0 + 512, 256), (b0 + 384, 128),
            (d0 + 256, 256), (a0 + 256, 256), (b0 + 256, 128),
            (c0, 768))
    wt = jnp.concatenate([w_in_l[:, s0:s0 + w] for s0, w in segs], axis=1).T.astype(BF16)
    wvc = w_in_l[:, c0 + 768:c0 + 1152].astype(BF16)
    qs64 = HEAD_DIM ** -0.5 * LOG2E
    qs32 = DIFF_HALF ** -0.5 * LOG2E
    ones = jnp.ones((VT_ROWS,), F32)
    gs = jnp.concatenate([
        jnp.tile(qkn_a[0] * qs64, 4), jnp.tile(qkn_b[0] * qs64, 4), jnp.tile(qkn_d[0] * qs32, 8),
        ones,
        jnp.tile(qkn_d[1], 8), jnp.tile(qkn_a[1], 4), jnp.tile(qkn_b[1], 2),
        jnp.tile(qkn_c[0] * qs64, 6), jnp.tile(qkn_c[1], 6),
    ]).astype(F32).reshape(WT_ROWS, 1)
    return wt, wvc, gs


def _ffn_weights(w_up_l, conv_w_l, conv_b_l, w_down_l):
    d = w_up_l.shape[0]
    wup = w_up_l.reshape(d, 2 * N_FF_CHUNKS, FF_CHUNK).transpose(1, 0, 2).astype(BF16)
    cw = jnp.concatenate([conv_w_l, conv_b_l[None], jnp.zeros((4, 2 * D_FF), F32)], axis=0)
    cw = cw.reshape(8, 2 * N_FF_CHUNKS, FF_CHUNK).transpose(1, 0, 2)
    wdn = w_down_l.reshape(N_FF_CHUNKS, FF_CHUNK, d).astype(BF16)
    return wup, cw, wdn


def _layer(x, mod, l, p, tables):
    tbl_a, tbl_d, tbl_b, tbl_c = tables
    wt, wvc, gs = _in_weights(p["w_in"][l], p["qkn_a"][l], p["qkn_b"][l], p["qkn_c"][l], p["qkn_d"][l])
    qT, vT, k, qkv_c = _in_call(x, mod, p["norm_attn_g"][l].reshape(1, -1), wt, wvc, gs)
    lambda_init = 0.8 - 0.6 * math.exp(-0.3 * l)
    o_a = _a_call(qT, k, vT, tbl_a, l)
    o_b = _b_call(p["sink_b"][l], qT, k, vT, tbl_b)
    o_c, lse_c = [], []
    for g, (_, r) in enumerate(DIL_PAIRS):
        o, lse = _c_call(qkv_c, tbl_c[g], g, r)
        o_c.append(o)
        lse_c.append(lse)
    o_d = _d_call(p["lam_d"][l], p["subln_d"][l].reshape(HEAD_DIM, 1), qT, k, vT, tbl_d, lambda_init)
    x = _out_call(x, mod, o_a, o_b, o_d, o_c, lse_c, p["w_out"][l].astype(BF16))
    wup, cw, wdn = _ffn_weights(p["w_up"][l], p["conv_w"][l], p["conv_b"][l], p["w_down"][l])
    return _ffn_call(x, mod, p["norm_ffn_g"][l].reshape(1, -1), wup, cw, wdn)


def _trunk(x, c, p, depth=DEPTH):
    bt, _, d = x.shape
    mod_all = _ada_call(c, p["w_ada"], p["b_ada"]).reshape(DEPTH, bt, 6, d)
    tables = (_bias_tables_a(p["rpb_a"]),) + _bias_tables_t5(p["t5_table"])
    for l in range(depth):
        x = _layer(x, mod_all[l], l, p, tables)
    return x


def kernel(x_prompt, x_sample, c_prompt, c_sample, norm_attn_g, norm_ffn_g, w_ada, b_ada, w_in, qkn_a, qkn_b, qkn_c, qkn_d, rpb_a, sink_b, t5_table, lam_d, subln_d, w_out, w_up, conv_w, conv_b, w_down):
    p = dict(norm_attn_g=norm_attn_g, norm_ffn_g=norm_ffn_g, w_ada=w_ada, b_ada=b_ada, w_in=w_in,
             qkn_a=qkn_a, qkn_b=qkn_b, qkn_c=qkn_c, qkn_d=qkn_d, rpb_a=rpb_a, sink_b=sink_b,
             t5_table=t5_table, lam_d=lam_d, subln_d=subln_d, w_out=w_out, w_up=w_up,
             conv_w=conv_w, conv_b=conv_b, w_down=w_down)
    nb = x_prompt.shape[0]
    x = jnp.concatenate([x_prompt, x_sample], axis=0)
    c = jnp.concatenate([c_prompt, c_sample], axis=0)
    y = _trunk(x, c, p)
    return (y[:nb], y[nb:])
```

```python
import functools
import math

import numpy as np
import jax
import jax.numpy as jnp
from jax import lax
from jax.experimental import pallas as pl
from jax.experimental.pallas import tpu as pltpu

D_MODEL = 1024
SEQ = 2048
DEPTH = 4
HEAD_DIM = 64
GRID_W = 64
NA_HEADS = 4
NA_WIN_H = 8
NA_WIN_W = 16
SW_HEADS = 4
SW_KV_HEADS = 2
SW_HALF_WINDOW = 128
DIL_PAIRS = ((128, 1), (512, 4), (2048, 16))
DIFF_HEADS = 4
DIFF_HALF = HEAD_DIM // 2
T5_BUCKETS = 32
T5_MAX_DIST = 128
T5_B0 = 0
T5_C0 = 4
T5_D0 = 10
D_FF = 2816
RMS_EPS = 1e-6
NEG_INF = -1e30
LOG2E = 1.4426950408889634

F32 = jnp.float32
BF16 = jnp.bfloat16

VMEM_LIMIT_BYTES = 56 * 1024 * 1024
TN_IN = 512
TN_OUT = 512
TN_FFN = 512
FF_CHUNK = 256
N_FF_CHUNKS = D_FF // FF_CHUNK
HALO = 8
TQ_D = 256
KB_D = 256
TQ_B = 128
TQ_C = 128
A_PAIR = 2 * GRID_W
A_WIN = 10 * GRID_W

QT_ROWS = 768
VT_ROWS = 640
K_ROWS = 640
C_QK_ROWS = 768
WT_ROWS = QT_ROWS + VT_ROWS + K_ROWS + C_QK_ROWS
C_WIDTH = 1152


def _cparams(sem):
    return pltpu.CompilerParams(dimension_semantics=sem, vmem_limit_bytes=VMEM_LIMIT_BYTES)


def _const_spec(shape):
    nd = len(shape)
    return pl.BlockSpec(shape, lambda *_: (0,) * nd, pipeline_mode=pl.Buffered(1))


def _ada_kernel(c_ref, w_ref, b_ref, o_ref):
    c = c_ref[...]
    sc = (c * (1.0 / (1.0 + jnp.exp(-c)))).astype(BF16)
    o_ref[0] = jnp.dot(sc, w_ref[0].astype(BF16), preferred_element_type=F32) + b_ref[0]


def _ada_call(c, w_ada, b_ada):
    bt = c.shape[0]
    ncol = 6 * D_MODEL
    cb = 1024
    return pl.pallas_call(
        _ada_kernel,
        grid=(DEPTH, ncol // cb),
        in_specs=[
            pl.BlockSpec((bt, D_MODEL), lambda l, j: (0, 0)),
            pl.BlockSpec((1, D_MODEL, cb), lambda l, j: (l, 0, j)),
            pl.BlockSpec((1, 1, cb), lambda l, j: (l, 0, j)),
        ],
        out_specs=pl.BlockSpec((1, bt, cb), lambda l, j: (l, 0, j)),
        out_shape=jax.ShapeDtypeStruct((DEPTH, bt, ncol), F32),
        compiler_params=_cparams(("arbitrary", "arbitrary")),
        name="ada_mod",
    )(c, w_ada, b_ada.reshape(DEPTH, 1, ncol))


def _mod_norm(x, g, scale, shift):
    ms = jnp.mean(x * x, axis=-1, keepdims=True)
    return x * lax.rsqrt(ms + RMS_EPS) * (g * (1.0 + scale)) + shift


def _in_kernel(x_ref, mod_ref, g_ref, wt_ref, wvc_ref, gs_ref, qT_ref, vT_ref, k_ref, c_ref, h_scr):
    tn = x_ref.shape[1]
    h = _mod_norm(x_ref[0], g_ref[...], mod_ref[0, 1:2, :], mod_ref[0, 0:1, :])
    h_scr[...] = h.astype(BF16)
    hb = h_scr[...]

    def proj_t(r0, rows):
        return lax.dot_general(wt_ref[r0:r0 + rows, :], hb, (((1,), (1,)), ((), ())),
                               preferred_element_type=F32)

    def head_norm(y, r0, grp):
        rows = y.shape[0]
        y3 = y.reshape(rows // grp, grp, tn)
        ms = jnp.sum(y3 * y3, axis=1, keepdims=True) * (1.0 / grp)
        gs = gs_ref[r0:r0 + rows, :].reshape(rows // grp, grp, 1)
        return (y3 * lax.rsqrt(ms + RMS_EPS) * gs).reshape(rows, tn)

    for r0, grp in ((0, HEAD_DIM), (256, HEAD_DIM), (512, DIFF_HALF)):
        qT_ref[0, r0:r0 + 256, :] = head_norm(proj_t(r0, 256), r0, grp).astype(BF16)
    base = QT_ROWS
    for r0, rows in ((0, 256), (256, 256), (512, 128)):
        vT_ref[0, r0:r0 + rows, :] = proj_t(base + r0, rows).astype(BF16)
    base = QT_ROWS + VT_ROWS
    for r0, rows, grp in ((0, 256, DIFF_HALF), (256, 256, HEAD_DIM), (512, 128, HEAD_DIM)):
        kt = head_norm(proj_t(base + r0, rows), base + r0, grp)
        k_ref[0, :, r0:r0 + rows] = kt.T.astype(BF16)
    base = QT_ROWS + VT_ROWS + K_ROWS
    for r0 in (0, 256, 512):
        ct = head_norm(proj_t(base + r0, 256), base + r0, HEAD_DIM)
        c_ref[0, :, r0:r0 + 256] = ct.T
    c_ref[0, :, C_QK_ROWS:C_WIDTH] = jnp.dot(hb, wvc_ref[...], preferred_element_type=F32)


def _in_call(x, mod, g, wt, wvc, gs):
    bt, n, d = x.shape
    tn = TN_IN
    return pl.pallas_call(
        _in_kernel,
        grid=(bt, n // tn),
        in_specs=[
            pl.BlockSpec((1, tn, d), lambda b, t: (b, t, 0)),
            pl.BlockSpec((1, 6, d), lambda b, t: (b, 0, 0)),
            _const_spec((1, d)),
            _const_spec((WT_ROWS, d)),
            _const_spec((d, C_WIDTH - C_QK_ROWS)),
            _const_spec((WT_ROWS, 1)),
        ],
        out_specs=[
            pl.BlockSpec((1, QT_ROWS, tn), lambda b, t: (b, 0, t)),
            pl.BlockSpec((1, VT_ROWS, tn), lambda b, t: (b, 0, t)),
            pl.BlockSpec((1, tn, K_ROWS), lambda b, t: (b, t, 0)),
            pl.BlockSpec((1, tn, C_WIDTH), lambda b, t: (b, t, 0)),
        ],
        out_shape=[
            jax.ShapeDtypeStruct((bt, QT_ROWS, n), BF16),
            jax.ShapeDtypeStruct((bt, VT_ROWS, n), BF16),
            jax.ShapeDtypeStruct((bt, n, K_ROWS), BF16),
            jax.ShapeDtypeStruct((bt, n, C_WIDTH), F32),
        ],
        scratch_shapes=[pltpu.VMEM((tn, d), BF16)],
        compiler_params=_cparams(("parallel", "parallel")),
        name="in_proj",
    )(x, mod, g, wt, wvc, gs)


def _d_kernel(lam_ref, sub_ref, q_ref, k_ref, v_ref, tbl_ref, o_ref, s_scr, p_scr, *, lambda_init):
    t = pl.program_id(1)
    tq = q_ref.shape[2]
    nkb = k_ref.shape[1] // KB_D
    qT = q_ref[0]
    rowgrp = lax.broadcasted_iota(jnp.int32, qT.shape, 0) // DIFF_HALF
    lp = lam_ref[...]
    lam = (jnp.exp(jnp.sum(lp[0:1] * lp[1:2], axis=-1, keepdims=True))
           - jnp.exp(jnp.sum(lp[2:3] * lp[3:4], axis=-1, keepdims=True)) + lambda_init)

    blocks = []
    for d in range(-1, nkb - 1):
        jb = lax.rem(t + d + nkb, nkb)
        blocks.append((d <= 1, pl.multiple_of(jb * KB_D, KB_D), jnp.clip(jb - t, -2, 2) + 2))

    nblk = len(blocks)
    vmax = lambda s: jnp.max(jnp.max(s.reshape(KB_D // 8, 8, tq), axis=0), axis=0, keepdims=True)

    def scores(h, pos, st):
        near, k0, idx = blocks[pos]
        kb = k_ref[0, pl.ds(k0, KB_D), :]
        for i in range(2):
            slot = 2 * (h % 2) + i
            s = jnp.dot(kb, st["q"][i], preferred_element_type=F32)
            if near:
                s = s + tbl_ref[h, idx]
                s_scr[slot, pos] = s
                st["m"][i] = jnp.maximum(st["m"][i], vmax(s))
            else:
                s_scr[slot, pos] = s
                st["m"][i] = jnp.maximum(st["m"][i], vmax(s) + tbl_ref[h, idx, 0:1, :])

    def probs(h, pos, st):
        near, k0, idx = blocks[pos]
        for i in range(2):
            slot = 2 * (h % 2) + i
            shift = st["m"][i] if near else st["m"][i] - tbl_ref[h, idx, 0:1, :]
            p_scr[slot, pl.ds(k0, KB_D), :] = jnp.exp2(s_scr[slot, pos] - shift).astype(BF16)

    def new_state(h):
        return {"q": [jnp.where(rowgrp == 2 * h + i, qT, jnp.zeros_like(qT)) for i in range(2)],
                "m": [jnp.full((1, tq), -jnp.inf, F32)] * 2}

    ones = jnp.ones((16, k_ref.shape[1]), BF16)

    def finish(h, st):
        vh = jnp.concatenate([v_ref[0, h * HEAD_DIM:(h + 1) * HEAD_DIM, :], ones], axis=0)
        o = []
        for i in range(2):
            ol = jnp.dot(vh, p_scr[2 * (h % 2) + i], preferred_element_type=F32)
            o.append(ol[:HEAD_DIM] * (1.0 / ol[HEAD_DIM:HEAD_DIM + 1]))
        oh = o[0] - lam * o[1]
        ms = jnp.mean(oh * oh, axis=0, keepdims=True)
        return oh * lax.rsqrt(ms + RMS_EPS) * sub_ref[...] * (1.0 - lambda_init)

    st = new_state(0)
    for pos in range(nblk):
        scores(0, pos, st)
    parts = []
    for h in range(DIFF_HEADS):
        nxt = new_state(h + 1) if h + 1 < DIFF_HEADS else None
        for pos in range(nblk):
            if nxt is not None:
                scores(h + 1, pos, nxt)
            probs(h, pos, st)
        parts.append(finish(h, st))
        st = nxt
    o_ref[0] = jnp.concatenate(parts, axis=0).T.astype(BF16)


def _d_call(lam_p, subln, qT, k, vT, tbl, lambda_init):
    bt, _, n = qT.shape
    tq = TQ_D
    return pl.pallas_call(
        functools.partial(_d_kernel, lambda_init=lambda_init),
        grid=(bt, n // tq),
        in_specs=[
            _const_spec((4, DIFF_HALF)),
            _const_spec((HEAD_DIM, 1)),
            pl.BlockSpec((1, 256, tq), lambda b, t: (b, 2, t)),
            pl.BlockSpec((1, n, 256), lambda b, t: (b, 0, 0)),
            pl.BlockSpec((1, 256, n), lambda b, t: (b, 0, 0)),
            _const_spec((DIFF_HEADS, 5, KB_D, tq)),
        ],
        out_specs=pl.BlockSpec((1, tq, 256), lambda b, t: (b, t, 0)),
        out_shape=jax.ShapeDtypeStruct((bt, n, 256), BF16),
        scratch_shapes=[
            pltpu.VMEM((4, n // KB_D, KB_D, tq), F32),
            pltpu.VMEM((4, n, tq), BF16),
        ],
        compiler_params=_cparams(("parallel", "arbitrary")),
        name="mixer_d",
    )(lam_p, subln, qT, k, vT, tbl)


def _b_kernel(sink_ref, q_ref, k_ref, v_ref, bias_ref, o_ref):
    n = k_ref.shape[1]
    tq = TQ_B
    nt = n // tq
    grp = SW_HEADS // SW_KV_HEADS
    zero = jnp.zeros((HEAD_DIM, tq), BF16)
    sinks = [sink_ref[h] * LOG2E for h in range(SW_HEADS)]
    group = 2
    for t0 in range(0, nt, group):
        chains = []
        for t in range(t0, t0 + group):
            qT = q_ref[0, :, t * tq:(t + 1) * tq]
            qcat = []
            for h in range(SW_HEADS):
                qh = qT[h * HEAD_DIM:(h + 1) * HEAD_DIM]
                qcat.append(jnp.concatenate([qh, zero] if h // grp == 0 else [zero, qh], axis=0))
            qcat = jnp.concatenate(qcat, axis=1)
            blocks = [j for j in (-1, 0, 1) if 0 <= t + j < nt]
            s_all = [jnp.dot(k_ref[0, (t + j) * tq:(t + j + 1) * tq, :], qcat,
                             preferred_element_type=F32) for j in blocks]
            for h in range(SW_HEADS):
                chains.append((t, h, [(s[:, h * tq:(h + 1) * tq] + bias_ref[h, j + 1], t + j)
                                      for s, j in zip(s_all, blocks)]))
        m_l = []
        for t, h, ss in chains:
            m = jnp.full((1, tq), sinks[h], F32)
            for s, _ in ss:
                m = jnp.maximum(m, jnp.max(s, axis=0, keepdims=True))
            m_l.append(m)
        p_l = [[jnp.exp2(s - m) for s, _ in ss] for (t, h, ss), m in zip(chains, m_l)]
        outs = []
        for (t, h, ss), m, ps in zip(chains, m_l, p_l):
            kv = h // grp
            l = jnp.exp2(sinks[h] - m)
            o = jnp.zeros((HEAD_DIM, tq), F32)
            for p, (_, kb) in zip(ps, ss):
                l = l + jnp.sum(p, axis=0, keepdims=True)
                vh = v_ref[0, kv * HEAD_DIM:(kv + 1) * HEAD_DIM, kb * tq:(kb + 1) * tq]
                o = o + jnp.dot(vh, p.astype(BF16), preferred_element_type=F32)
            outs.append(o * (1.0 / l))
        for i, t in enumerate(range(t0, t0 + group)):
            o_ref[0, t * tq:(t + 1) * tq, :] = jnp.concatenate(
                outs[i * SW_HEADS:(i + 1) * SW_HEADS], axis=0).T.astype(BF16)


def _b_call(sink, qT, k, vT, bias):
    bt, _, n = qT.shape
    return pl.pallas_call(
        _b_kernel,
        grid=(bt,),
        in_specs=[
            pl.BlockSpec(memory_space=pltpu.SMEM),
            pl.BlockSpec((1, 256, n), lambda b: (b, 1, 0)),
            pl.BlockSpec((1, n, 128), lambda b: (b, 0, 4)),
            pl.BlockSpec((1, 128, n), lambda b: (b, 4, 0)),
            _const_spec((SW_HEADS, 3, TQ_B, TQ_B)),
        ],
        out_specs=pl.BlockSpec((1, n, 256), lambda b: (b, 0, 0)),
        out_shape=jax.ShapeDtypeStruct((bt, n, 256), BF16),
        compiler_params=_cparams(("parallel",)),
        name="mixer_b",
    )(sink, qT, k, vT, bias)


def _a_kernel(q_ref, k_ref, v_ref, bias_ref, o_ref):
    n = k_ref.shape[1]
    rows = n // GRID_W
    npair = n // A_PAIR
    kh = min(NA_WIN_H, rows)
    rowhead = lax.broadcasted_iota(jnp.int32, (NA_HEADS * HEAD_DIM, A_PAIR), 0) // HEAD_DIM
    group = 2
    for g0 in range(0, npair, group):
        geo, s_all = [], []
        for rp in range(g0, g0 + group):
            r = 2 * rp
            ws = min(max(r - kh // 2, 0), rows - kh, rows - A_WIN // GRID_W)
            var = rp if rp < 2 else (2 if rp < npair - 2 else rp - (npair - 5))
            q0, k0 = rp * A_PAIR, ws * GRID_W
            qT = q_ref[0, :, q0:q0 + A_PAIR]
            qcat = jnp.concatenate([jnp.where(rowhead == h, qT, jnp.zeros_like(qT))
                                    for h in range(NA_HEADS)], axis=1)
            s_all.append(jnp.dot(k_ref[0, k0:k0 + A_WIN, :], qcat, preferred_element_type=F32))
            geo.append((q0, k0, var))
        chains = [(i, h) for i in range(group) for h in range(NA_HEADS)]
        s_l = [s_all[i][:, h * A_PAIR:(h + 1) * A_PAIR] + bias_ref[h, geo[i][2]] for i, h in chains]
        m_l = [jnp.max(s, axis=0, keepdims=True) for s in s_l]
        p_l = [jnp.exp2(s - m) for s, m in zip(s_l, m_l)]
        l_l = [jnp.sum(p, axis=0, keepdims=True) for p in p_l]
        o_l = [jnp.dot(v_ref[0, h * HEAD_DIM:(h + 1) * HEAD_DIM, geo[i][1]:geo[i][1] + A_WIN],
                       p.astype(BF16), preferred_element_type=F32) * (1.0 / l)
               for (i, h), p, l in zip(chains, p_l, l_l)]
        for i in range(group):
            q0 = geo[i][0]
            o_ref[0, q0:q0 + A_PAIR, :] = jnp.concatenate(
                o_l[i * NA_HEADS:(i + 1) * NA_HEADS], axis=0).T.astype(BF16)


def _a_call(qT, k, vT, bias, layer):
    bt, _, n = qT.shape
    return pl.pallas_call(
        _a_kernel,
        grid=(bt,),
        in_specs=[
            pl.BlockSpec((1, 256, n), lambda b: (b, 0, 0)),
            pl.BlockSpec((1, n, 256), lambda b: (b, 0, 1)),
            pl.BlockSpec((1, 256, n), lambda b: (b, 1, 0)),
            pl.BlockSpec((NA_HEADS, 5, A_WIN, A_PAIR), lambda b: (layer, 0, 0, 0),
                         pipeline_mode=pl.Buffered(1)),
        ],
        out_specs=pl.BlockSpec((1, n, 256), lambda b: (b, 0, 0)),
        out_shape=jax.ShapeDtypeStruct((bt, n, 256), BF16),
        compiler_params=_cparams(("parallel",)),
        name="mixer_a",
    )(qT, k, vT, bias)


def _c_kernel(q_ref, k_ref, v_ref, bias_ref, o_ref, lse_ref, sub_scr, o_scr, lse_scr, *, dil):
    n = q_ref.shape[1]
    sub_len = n // dil
    win = min(2 * TQ_C, sub_len)
    nqb = sub_len // TQ_C
    lane_head = lax.broadcasted_iota(jnp.int32, (TQ_C, 2 * HEAD_DIM), 1) // HEAD_DIM

    def gather(p, carry):
        src = pl.ds(p, sub_len, stride=dil)
        dst = pl.ds(pl.multiple_of(p * sub_len, sub_len), sub_len)
        for i, ref in enumerate((q_ref, k_ref, v_ref)):
            sub_scr[i, dst, :] = ref[0, src, :].astype(BF16)
        return carry

    if dil > 1:
        lax.fori_loop(0, dil, gather, 0)
    else:
        for i, ref in enumerate((q_ref, k_ref, v_ref)):
            sub_scr[i] = ref[0].astype(BF16)

    group = 4
    for g0 in range(0, n // TQ_C, group):
        geo = []
        for item in range(g0, g0 + group):
            p, qb = divmod(item, nqb)
            q0 = p * sub_len + qb * TQ_C
            start = p * sub_len + min(max(qb * TQ_C - TQ_C // 2, 0), sub_len - win)
            var = 0 if qb == 0 else (2 if qb == nqb - 1 else 1)
            geo.extend((q0, start, var, j) for j in range(2))
        s_l = []
        for q0, start, var, j in geo:
            qblk = sub_scr[0, q0:q0 + TQ_C, :]
            qm = jnp.where(lane_head == j, qblk, jnp.zeros_like(qblk))
            s = lax.dot_general(qm, sub_scr[1, start:start + win, :], (((1,), (1,)), ((), ())),
                                preferred_element_type=F32)
            s_l.append(s + bias_ref[j, var])
        m_l = [jnp.max(s, axis=-1, keepdims=True) for s in s_l]
        p_l = [jnp.exp2(s - m) for s, m in zip(s_l, m_l)]
        l_l = [jnp.sum(pr, axis=-1, keepdims=True) for pr in p_l]
        o_l = [jnp.dot(pr.astype(BF16), sub_scr[2, start:start + win, :], preferred_element_type=F32)
               for pr, (q0, start, var, j) in zip(p_l, geo)]
        for i in range(0, len(geo), 2):
            q0 = geo[i][0]
            o0, o1 = o_l[i] * (1.0 / l_l[i]), o_l[i + 1] * (1.0 / l_l[i + 1])
            e0 = jnp.broadcast_to(m_l[i] + jnp.log2(l_l[i]), (TQ_C, 2 * HEAD_DIM))
            e1 = jnp.broadcast_to(m_l[i + 1] + jnp.log2(l_l[i + 1]), (TQ_C, 2 * HEAD_DIM))
            o_scr[q0:q0 + TQ_C, :] = jnp.where(lane_head == 0, o0, o1)
            lse_scr[q0:q0 + TQ_C, :] = jnp.where(lane_head == 0, e0, e1)

    def scatter(p, carry):
        dst = pl.ds(p, sub_len, stride=dil)
        src = pl.ds(pl.multiple_of(p * sub_len, sub_len), sub_len)
        o_ref[0, dst, :] = o_scr[src, :]
        lse_ref[0, dst, :] = lse_scr[src, :]
        return carry

    if dil > 1:
        lax.fori_loop(0, dil, scatter, 0)
    else:
        o_ref[0] = o_scr[...]
        lse_ref[0] = lse_scr[...]


def _c_call(qkv, bias, g, dil):
    bt, n, _ = qkv.shape
    sub_len = n // dil
    win = min(2 * TQ_C, sub_len)
    lanes = 2 * HEAD_DIM
    return pl.pallas_call(
        functools.partial(_c_kernel, dil=dil),
        grid=(bt,),
        in_specs=[
            pl.BlockSpec((1, n, lanes), lambda b: (b, 0, g)),
            pl.BlockSpec((1, n, lanes), lambda b: (b, 0, 3 + g)),
            pl.BlockSpec((1, n, lanes), lambda b: (b, 0, 6 + g)),
            _const_spec((2, 3, TQ_C, win)),
        ],
        out_specs=[
            pl.BlockSpec((1, n, lanes), lambda b: (b, 0, 0)),
            pl.BlockSpec((1, n, lanes), lambda b: (b, 0, 0)),
        ],
        out_shape=[
            jax.ShapeDtypeStruct((bt, n, lanes), F32),
            jax.ShapeDtypeStruct((bt, n, lanes), F32),
        ],
        scratch_shapes=[
            pltpu.VMEM((3, n, lanes), BF16),
            pltpu.VMEM((n, lanes), F32),
            pltpu.VMEM((n, lanes), F32),
        ],
        compiler_params=_cparams(("parallel",)),
        name=f"mixer_c{g}",
    )(qkv, qkv, qkv, bias)


def _out_kernel(x_ref, mod_ref, oa_ref, ob_ref, od_ref, oc0_ref, oc1_ref, oc2_ref,
                l0_ref, l1_ref, l2_ref, w_ref, o_ref):
    l0, l1, l2 = l0_ref[0], l1_ref[0], l2_ref[0]
    mx = jnp.maximum(jnp.maximum(l0, l1), l2)
    e0, e1, e2 = jnp.exp2(l0 - mx), jnp.exp2(l1 - mx), jnp.exp2(l2 - mx)
    inv = 1.0 / (e0 + e1 + e2)
    y = jnp.dot(oa_ref[0], w_ref[0:256, :], preferred_element_type=F32)
    y += jnp.dot(ob_ref[0], w_ref[256:512, :], preferred_element_type=F32)
    y += jnp.dot((oc0_ref[0] * (e0 * inv)).astype(BF16), w_ref[512:640, :], preferred_element_type=F32)
    y += jnp.dot((oc1_ref[0] * (e1 * inv)).astype(BF16), w_ref[640:768, :], preferred_element_type=F32)
    y += jnp.dot((oc2_ref[0] * (e2 * inv)).astype(BF16), w_ref[768:896, :], preferred_element_type=F32)
    y += jnp.dot(od_ref[0], w_ref[896:1152, :], preferred_element_type=F32)
    o_ref[0] = x_ref[0] + mod_ref[0, 2:3, :] * y


def _out_call(x, mod, o_a, o_b, o_d, o_c, lse_c, w_out):
    bt, n, d = x.shape
    tn = TN_OUT
    tok = lambda w: pl.BlockSpec((1, tn, w), lambda b, t: (b, t, 0))
    return pl.pallas_call(
        _out_kernel,
        grid=(bt, n // tn),
        in_specs=[tok(d), pl.BlockSpec((1, 6, d), lambda b, t: (b, 0, 0)),
                  tok(256), tok(256), tok(256), tok(128), tok(128), tok(128),
                  tok(128), tok(128), tok(128), _const_spec(w_out.shape)],
        out_specs=tok(d),
        out_shape=jax.ShapeDtypeStruct((bt, n, d), F32),
        compiler_params=_cparams(("parallel", "parallel")),
        name="out_proj",
    )(x, mod, o_a, o_b, o_d, *o_c, *lse_c, w_out)


def _ffn_kernel(xp_ref, x_ref, xn_ref, mod_ref, g_ref, wup_ref, cw_ref, wdn_ref, o_ref, h_scr, acc_scr):
    t = pl.program_id(1)
    nt = pl.num_programs(1)
    tn = x_ref.shape[1]
    ext = tn + 2 * HALO
    xe = jnp.concatenate([xp_ref[0], x_ref[0], xn_ref[0]], axis=0)
    h = _mod_norm(xe, g_ref[...], mod_ref[0, 4:5, :], mod_ref[0, 3:4, :])
    row = lax.broadcasted_iota(jnp.int32, (ext, 1), 0)
    keep = jnp.logical_and(jnp.logical_or(row >= HALO, t > 0),
                           jnp.logical_or(row < HALO + tn, t < nt - 1))
    h_scr[...] = jnp.where(keep, h, 0.0).astype(BF16)
    acc_scr[...] = jnp.zeros_like(acc_scr)

    def conv(u, cw):
        prev = pltpu.roll(u, 1, 0)[HALO:HALO + tn]
        nxt = pltpu.roll(u, ext - 1, 0)[HALO:HALO + tn]
        return prev * cw[0:1] + u[HALO:HALO + tn] * cw[1:2] + nxt * cw[2:3] + cw[3:4]

    hb = h_scr[...]
    up = lambda c: (jnp.dot(hb, wup_ref[c], preferred_element_type=F32),
                    jnp.dot(hb, wup_ref[N_FF_CHUNKS + c], preferred_element_type=F32))
    u_val, u_gate = up(0)
    for c in range(N_FF_CHUNKS):
        nxt = up(c + 1) if c + 1 < N_FF_CHUNKS else None
        val = conv(u_val, cw_ref[c])
        gate = conv(u_gate, cw_ref[N_FF_CHUNKS + c])
        act = (val * gate * (1.0 / (1.0 + jnp.exp(-gate)))).astype(BF16)
        acc_scr[...] += jnp.dot(act, wdn_ref[c], preferred_element_type=F32)
        if nxt is not None:
            u_val, u_gate = nxt
    o_ref[0] = x_ref[0] + mod_ref[0, 5:6, :] * acc_scr[...]


def _ffn_call(x, mod, g, wup, cw, wdn):
    bt, n, d = x.shape
    tn = TN_FFN
    nb = tn // HALO
    nhalo = n // HALO
    return pl.pallas_call(
        _ffn_kernel,
        grid=(bt, n // tn),
        in_specs=[
            pl.BlockSpec((1, HALO, d), lambda b, t: (b, jnp.maximum(t * nb - 1, 0), 0)),
            pl.BlockSpec((1, tn, d), lambda b, t: (b, t, 0)),
            pl.BlockSpec((1, HALO, d), lambda b, t: (b, jnp.minimum((t + 1) * nb, nhalo - 1), 0)),
            pl.BlockSpec((1, 6, d), lambda b, t: (b, 0, 0)),
            _const_spec((1, d)),
            _const_spec(wup.shape),
            _const_spec(cw.shape),
            _const_spec(wdn.shape),
        ],
        out_specs=pl.BlockSpec((1, tn, d), lambda b, t: (b, t, 0)),
        out_shape=jax.ShapeDtypeStruct((bt, n, d), F32),
        scratch_shapes=[pltpu.VMEM((tn + 2 * HALO, d), BF16), pltpu.VMEM((tn, d), F32)],
        compiler_params=_cparams(("parallel", "parallel")),
        name="ffn",
    )(x, x, x, mod, g, wup, cw, wdn)


def _t5_bucket(rel):
    nb = T5_BUCKETS // 2
    max_exact = nb // 2
    base = jnp.where(rel > 0, nb, 0)
    n = jnp.abs(rel)
    nf = jnp.maximum(n, 1).astype(jnp.float32)
    large = max_exact + (jnp.log(nf / max_exact) / math.log(T5_MAX_DIST / max_exact)
                         * (nb - max_exact)).astype(jnp.int32)
    large = jnp.minimum(large, nb - 1)
    return base + jnp.where(n < max_exact, n, large)


def _t5_kernel(tab_ref, bkt_ref, o_ref, *, head0):
    h = head0 + pl.program_id(0)
    for v in range(bkt_ref.shape[0]):
        bkt = bkt_ref[v]
        acc = jnp.full(bkt.shape, NEG_INF, F32)
        for b in range(T5_BUCKETS):
            acc = jnp.where(bkt == b, tab_ref[b, h] * LOG2E, acc)
        o_ref[0, v] = acc


def _t5_bias(table, rel, mask, head0, nheads):
    bkt = jnp.where(jnp.asarray(mask), _t5_bucket(jnp.asarray(rel, dtype=jnp.int32)), -1)
    return pl.pallas_call(
        functools.partial(_t5_kernel, head0=head0),
        grid=(nheads,),
        in_specs=[pl.BlockSpec(memory_space=pltpu.SMEM), _const_spec(bkt.shape)],
        out_specs=pl.BlockSpec((1,) + bkt.shape, lambda h: (h, 0, 0, 0)),
        out_shape=jax.ShapeDtypeStruct((nheads,) + bkt.shape, F32),
        compiler_params=_cparams(("arbitrary",)),
        name="t5_bias",
    )(table.astype(F32), bkt)


def _bias_tables_t5(t5_table):
    kk = np.arange(KB_D)[:, None]
    qq = np.arange(TQ_D)[None, :]
    far = 8 * T5_MAX_DIST
    rel_d = np.stack([np.full((KB_D, TQ_D), -far)] + [j * KB_D + kk - qq for j in (-1, 0, 1)]
                     + [np.full((KB_D, TQ_D), far)])
    tbl_d = _t5_bias(t5_table, rel_d, np.ones_like(rel_d, bool), T5_D0, DIFF_HEADS)
    kk = np.arange(TQ_B)[:, None]
    qq = np.arange(TQ_B)[None, :]
    rel_b = np.stack([j * TQ_B + kk - qq for j in (-1, 0, 1)])
    tbl_b = _t5_bias(t5_table, rel_b, np.abs(rel_b) <= SW_HALF_WINDOW, T5_B0, SW_HEADS)
    tbl_c = []
    for g, (w, r) in enumerate(DIL_PAIRS):
        sub_len = SEQ // r
        hw = w // (2 * r)
        win = min(2 * TQ_C, sub_len)
        qq = np.arange(TQ_C)[:, None]
        kk = np.arange(win)[None, :]
        offs = (0, -(TQ_C // 2), -(win - TQ_C))
        rel_c = np.stack([o + kk - qq for o in offs])
        tbl_c.append(_t5_bias(t5_table, rel_c * r, np.abs(rel_c) <= hw, T5_C0 + 2 * g, 2))
    return tbl_d, tbl_b, tbl_c


def _a_variants():
    rows = SEQ // GRID_W
    kh = min(NA_WIN_H, rows)
    wrows = A_WIN // GRID_W
    out = []
    for r in (0, 2, 4, rows - 4, rows - 2):
        ws = min(int(np.clip(r - kh // 2, 0, rows - kh)), rows - wrows)
        per_row = []
        for kr in range(wrows):
            pair = []
            for a in range(2):
                rsq = int(np.clip(r + a - kh // 2, 0, rows - kh))
                krow = ws + kr
                pair.append(krow - (r + a) + NA_WIN_H - 1 if rsq <= krow < rsq + kh else None)
            per_row.append(tuple(pair))
        out.append(tuple(per_row))
    return tuple(out)


def _abias_kernel(rpb_ref, dcm_ref, o_ref):
    i = pl.program_id(0)
    ncol = 2 * NA_WIN_W - 1
    dcm = dcm_ref[...]
    second = lax.broadcasted_iota(jnp.int32, dcm.shape, 1) >= GRID_W
    neg = jnp.full(dcm.shape, NEG_INF, F32)
    for v, per_row in enumerate(_a_variants()):
        for kr, (dr0, dr1) in enumerate(per_row):
            if dr0 is None and dr1 is None:
                o_ref[0, v, kr * GRID_W:(kr + 1) * GRID_W, :] = neg
                continue

            def body(t, acc, dr0=dr0, dr1=dr1):
                s0 = NEG_INF if dr0 is None else rpb_ref[i, dr0 * ncol + t] * LOG2E
                s1 = NEG_INF if dr1 is None else rpb_ref[i, dr1 * ncol + t] * LOG2E
                return jnp.where(dcm == t, jnp.where(second, s1, s0), acc)

            o_ref[0, v, kr * GRID_W:(kr + 1) * GRID_W, :] = lax.fori_loop(0, ncol, body, neg)


def _bias_tables_a(rpb_a):
    nl, nh, nr, nc = rpb_a.shape
    kc = np.arange(GRID_W)[:, None]
    c = np.tile(np.arange(GRID_W), 2)[None, :]
    cs = np.clip(c - NA_WIN_W // 2, 0, GRID_W - NA_WIN_W)
    dcm = np.where((kc >= cs) & (kc < cs + NA_WIN_W), np.clip(kc - c + NA_WIN_W - 1, 0, nc - 1), -1)
    return pl.pallas_call(
        _abias_kernel,
        grid=(nl * nh,),
        in_specs=[pl.BlockSpec(memory_space=pltpu.SMEM), _const_spec(dcm.shape)],
        out_specs=pl.BlockSpec((1, 5, A_WIN, A_PAIR), lambda i: (i, 0, 0, 0)),
        out_shape=jax.ShapeDtypeStruct((nl * nh, 5, A_WIN, A_PAIR), F32),
        compiler_params=_cparams(("arbitrary",)),
        name="rpb_bias",
    )(rpb_a.reshape(nl * nh, nr * nc).astype(F32), jnp.asarray(dcm, jnp.int32))


def _in_weights(w_in_l, qkn_a, qkn_b, qkn_c, qkn_d):
    a0, b0, c0, d0 = 0, 768, 1280, 2432
    segs = ((a0, 256), (b0, 256), (d0, 256),
            (d0 + 512, 256), (a0 + 512, 256), (b0 + 384, 128),
            (d0 + 256, 256), (a0 + 256, 256), (b0 + 256, 128),
            (c0, 768))
    wt = jnp.concatenate([w_in_l[:, s0:s0 + w] for s0, w in segs], axis=1).T.astype(BF16)
    wvc = w_in_l[:, c0 + 768:c0 + 1152].astype(BF16)
    qs64 = HEAD_DIM ** -0.5 * LOG2E
    qs32 = DIFF_HALF ** -0.5 * LOG2E
    ones = jnp.ones((VT_ROWS,), F32)
    gs = jnp.concatenate([
        jnp.tile(qkn_a[0] * qs64, 4), jnp.tile(qkn_b[0] * qs64, 4), jnp.tile(qkn_d[0] * qs32, 8),
        ones,
        jnp.tile(qkn_d[1], 8), jnp.tile(qkn_a[1], 4), jnp.tile(qkn_b[1], 2),
        jnp.tile(qkn_c[0] * qs64, 6), jnp.tile(qkn_c[1], 6),
    ]).astype(F32).reshape(WT_ROWS, 1)
    return wt, wvc, gs


def _ffn_weights(w_up_l, conv_w_l, conv_b_l, w_down_l):
    d = w_up_l.shape[0]
    wup = w_up_l.reshape(d, 2 * N_FF_CHUNKS, FF_CHUNK).transpose(1, 0, 2).astype(BF16)
    cw = jnp.concatenate([conv_w_l, conv_b_l[None], jnp.zeros((4, 2 * D_FF), F32)], axis=0)
    cw = cw.reshape(8, 2 * N_FF_CHUNKS, FF_CHUNK).transpose(1, 0, 2)
    wdn = w_down_l.reshape(N_FF_CHUNKS, FF_CHUNK, d).astype(BF16)
    return wup, cw, wdn


def _layer(x, mod, l, p, tables):
    tbl_a, tbl_d, tbl_b, tbl_c = tables
    wt, wvc, gs = _in_weights(p["w_in"][l], p["qkn_a"][l], p["qkn_b"][l], p["qkn_c"][l], p["qkn_d"][l])
    qT, vT, k, qkv_c = _in_call(x, mod, p["norm_attn_g"][l].reshape(1, -1), wt, wvc, gs)
    lambda_init = 0.8 - 0.6 * math.exp(-0.3 * l)
    o_a = _a_call(qT, k, vT, tbl_a, l)
    o_b = _b_call(p["sink_b"][l], qT, k, vT, tbl_b)
    o_c, lse_c = [], []
    for g, (_, r) in enumerate(DIL_PAIRS):
        o, lse = _c_call(qkv_c, tbl_c[g], g, r)
        o_c.append(o)
        lse_c.append(lse)
    o_d = _d_call(p["lam_d"][l], p["subln_d"][l].reshape(HEAD_DIM, 1), qT, k, vT, tbl_d, lambda_init)
    x = _out_call(x, mod, o_a, o_b, o_d, o_c, lse_c, p["w_out"][l].astype(BF16))
    wup, cw, wdn = _ffn_weights(p["w_up"][l], p["conv_w"][l], p["conv_b"][l], p["w_down"][l])
    return _ffn_call(x, mod, p["norm_ffn_g"][l].reshape(1, -1), wup, cw, wdn)


def _trunk(x, c, p, depth=DEPTH):
    bt, _, d = x.shape
    mod_all = _ada_call(c, p["w_ada"], p["b_ada"]).reshape(DEPTH, bt, 6, d)
    tables = (_bias_tables_a(p["rpb_a"]),) + _bias_tables_t5(p["t5_table"])
    for l in range(depth):
        x = _layer(x, mod_all[l], l, p, tables)
    return x


def kernel(x_prompt, x_sample, c_prompt, c_sample, norm_attn_g, norm_ffn_g, w_ada, b_ada, w_in, qkn_a, qkn_b, qkn_c, qkn_d, rpb_a, sink_b, t5_table, lam_d, subln_d, w_out, w_up, conv_w, conv_b, w_down):
    p = dict(norm_attn_g=norm_attn_g, norm_ffn_g=norm_ffn_g, w_ada=w_ada, b_ada=b_ada, w_in=w_in,
             qkn_a=qkn_a, qkn_b=qkn_b, qkn_c=qkn_c, qkn_d=qkn_d, rpb_a=rpb_a, sink_b=sink_b,
             t5_table=t5_table, lam_d=lam_d, subln_d=subln_d, w_out=w_out, w_up=w_up,
             conv_w=conv_w, conv_b=conv_b, w_down=w_down)
    nb = x_prompt.shape[0]
    x = jnp.concatenate([x_prompt, x_sample], axis=0)
    c = jnp.concatenate([c_prompt, c_sample], axis=0)
    y = _trunk(x, c, p)
    return (y[:nb], y[nb:])
```

```python
import functools
import math

import numpy as np
import jax
import jax.numpy as jnp
from jax import lax
from jax.experimental import pallas as pl
from jax.experimental.pallas import tpu as pltpu

D_MODEL = 1024
SEQ = 2048
DEPTH = 4
HEAD_DIM = 64
GRID_W = 64
NA_HEADS = 4
NA_WIN_H = 8
NA_WIN_W = 16
SW_HEADS = 4
SW_KV_HEADS = 2
SW_HALF_WINDOW = 128
DIL_PAIRS = ((128, 1), (512, 4), (2048, 16))
DIFF_HEADS = 4
DIFF_HALF = HEAD_DIM // 2
T5_BUCKETS = 32
T5_MAX_DIST = 128
T5_B0 = 0
T5_C0 = 4
T5_D0 = 10
D_FF = 2816
RMS_EPS = 1e-6
NEG_INF = -1e30
LOG2E = 1.4426950408889634

F32 = jnp.float32
BF16 = jnp.bfloat16

VMEM_LIMIT_BYTES = 56 * 1024 * 1024
TN_IN = 512
TN_OUT = 512
TN_FFN = 512
FF_CHUNK = 256
N_FF_CHUNKS = D_FF // FF_CHUNK
HALO = 8
TQ_D = 256
D_SAFE_BOUND = 60.0
KB_D = 256
TQ_B = 128
TQ_C = 128
A_PAIR = 2 * GRID_W
A_WIN = 10 * GRID_W

QT_ROWS = 768
VT_ROWS = 640
K_ROWS = 640
C_QK_ROWS = 768
WT_ROWS = QT_ROWS + VT_ROWS + K_ROWS + C_QK_ROWS
C_WIDTH = 1152


def _cparams(sem):
    return pltpu.CompilerParams(dimension_semantics=sem, vmem_limit_bytes=VMEM_LIMIT_BYTES)


def _const_spec(shape):
    nd = len(shape)
    return pl.BlockSpec(shape, lambda *_: (0,) * nd, pipeline_mode=pl.Buffered(1))


def _ada_kernel(c_ref, w_ref, b_ref, o_ref):
    c = c_ref[...]
    sc = (c * (1.0 / (1.0 + jnp.exp(-c)))).astype(BF16)
    o_ref[0] = jnp.dot(sc, w_ref[0].astype(BF16), preferred_element_type=F32) + b_ref[0]


def _ada_call(c, w_ada, b_ada):
    bt = c.shape[0]
    ncol = 6 * D_MODEL
    cb = 1024
    return pl.pallas_call(
        _ada_kernel,
        grid=(DEPTH, ncol // cb),
        in_specs=[
            pl.BlockSpec((bt, D_MODEL), lambda l, j: (0, 0)),
            pl.BlockSpec((1, D_MODEL, cb), lambda l, j: (l, 0, j)),
            pl.BlockSpec((1, 1, cb), lambda l, j: (l, 0, j)),
        ],
        out_specs=pl.BlockSpec((1, bt, cb), lambda l, j: (l, 0, j)),
        out_shape=jax.ShapeDtypeStruct((DEPTH, bt, ncol), F32),
        compiler_params=_cparams(("arbitrary", "arbitrary")),
        name="ada_mod",
    )(c, w_ada, b_ada.reshape(DEPTH, 1, ncol))


def _mod_norm(x, g, scale, shift):
    ms = jnp.mean(x * x, axis=-1, keepdims=True)
    return x * lax.rsqrt(ms + RMS_EPS) * (g * (1.0 + scale)) + shift


def _pick_group(refs, nb_first):
    if len(refs) == 1:
        return refs[0][0]
    return jnp.where(pl.program_id(0) < nb_first, refs[0][0], refs[1][0])


def _group_specs(xs, tn):
    d = xs[0].shape[-1]
    if len(xs) == 1:
        return [pl.BlockSpec((1, tn, d), lambda b, t: (b, t, 0))]
    nb = xs[0].shape[0]
    last = xs[0].shape[1] // tn - 1
    return [pl.BlockSpec((1, tn, d), lambda b, t: (jnp.minimum(b, nb - 1), jnp.where(b < nb, t, last), 0)),
            pl.BlockSpec((1, tn, d), lambda b, t: (jnp.maximum(b - nb, 0), jnp.where(b < nb, 0, t), 0))]


def _in_kernel(*refs, nsrc, nb_first):
    x_refs, (mod_ref, g_ref, wt_ref, wvc_ref, gs_ref, qT_ref, vT_ref, k_ref, c_ref, h_scr) = \
        refs[:nsrc], refs[nsrc:]
    tn = x_refs[0].shape[1]
    h = _mod_norm(_pick_group(x_refs, nb_first), g_ref[...], mod_ref[0, 1:2, :], mod_ref[0, 0:1, :])
    h_scr[...] = h.astype(BF16)
    hb = h_scr[...]

    def proj_t(r0, rows):
        return lax.dot_general(wt_ref[r0:r0 + rows, :], hb, (((1,), (1,)), ((), ())),
                               preferred_element_type=F32)

    def head_norm(y, r0, grp):
        rows = y.shape[0]
        y3 = y.reshape(rows // grp, grp, tn)
        ms = jnp.sum(y3 * y3, axis=1, keepdims=True) * (1.0 / grp)
        gs = gs_ref[r0:r0 + rows, :].reshape(rows // grp, grp, 1)
        return (y3 * lax.rsqrt(ms + RMS_EPS) * gs).reshape(rows, tn)

    for r0, grp in ((0, HEAD_DIM), (256, HEAD_DIM), (512, DIFF_HALF)):
        qT_ref[0, r0:r0 + 256, :] = head_norm(proj_t(r0, 256), r0, grp).astype(BF16)
    base = QT_ROWS
    for r0, rows in ((0, 256), (256, 256), (512, 128)):
        vT_ref[0, r0:r0 + rows, :] = proj_t(base + r0, rows).astype(BF16)
    base = QT_ROWS + VT_ROWS
    for r0, rows, grp in ((0, 256, DIFF_HALF), (256, 256, HEAD_DIM), (512, 128, HEAD_DIM)):
        kt = head_norm(proj_t(base + r0, rows), base + r0, grp)
        k_ref[0, :, r0:r0 + rows] = kt.T.astype(BF16)
    base = QT_ROWS + VT_ROWS + K_ROWS
    for r0 in (0, 256, 512):
        ct = head_norm(proj_t(base + r0, 256), base + r0, HEAD_DIM)
        c_ref[0, :, r0:r0 + 256] = ct.T
    c_ref[0, :, C_QK_ROWS:C_WIDTH] = jnp.dot(hb, wvc_ref[...], preferred_element_type=F32)


def _in_call(xs, mod, g, wt, wvc, gs):
    bt = sum(x.shape[0] for x in xs)
    _, n, d = xs[0].shape
    tn = TN_IN
    return pl.pallas_call(
        functools.partial(_in_kernel, nsrc=len(xs), nb_first=xs[0].shape[0]),
        grid=(bt, n // tn),
        in_specs=_group_specs(xs, tn) + [
            pl.BlockSpec((1, 6, d), lambda b, t: (b, 0, 0)),
            _const_spec((1, d)),
            _const_spec((WT_ROWS, d)),
            _const_spec((d, C_WIDTH - C_QK_ROWS)),
            _const_spec((WT_ROWS, 1)),
        ],
        out_specs=[
            pl.BlockSpec((1, QT_ROWS, tn), lambda b, t: (b, 0, t)),
            pl.BlockSpec((1, VT_ROWS, tn), lambda b, t: (b, 0, t)),
            pl.BlockSpec((1, tn, K_ROWS), lambda b, t: (b, t, 0)),
            pl.BlockSpec((1, tn, C_WIDTH), lambda b, t: (b, t, 0)),
        ],
        out_shape=[
            jax.ShapeDtypeStruct((bt, QT_ROWS, n), BF16),
            jax.ShapeDtypeStruct((bt, VT_ROWS, n), BF16),
            jax.ShapeDtypeStruct((bt, n, K_ROWS), BF16),
            jax.ShapeDtypeStruct((bt, n, C_WIDTH), F32),
        ],
        scratch_shapes=[pltpu.VMEM((tn, d), BF16)],
        compiler_params=_cparams(("arbitrary", "arbitrary")),
        name="in_proj",
    )(*xs, mod, g, wt, wvc, gs)


def _d_kernel(bound_ref, lam_ref, sub_ref, q_ref, k_ref, v_ref, tbl_ref, o_ref, s_scr, p_scr, *, lambda_init):
    t = pl.program_id(1)
    tq = q_ref.shape[2]
    nkb = k_ref.shape[1] // KB_D
    qT = q_ref[0]
    rowgrp = lax.broadcasted_iota(jnp.int32, qT.shape, 0) // DIFF_HALF
    lp = lam_ref[...]
    lam = (jnp.exp(jnp.sum(lp[0:1] * lp[1:2], axis=-1, keepdims=True))
           - jnp.exp(jnp.sum(lp[2:3] * lp[3:4], axis=-1, keepdims=True)) + lambda_init)

    blocks = []
    for d in range(-1, nkb - 1):
        jb = lax.rem(t + d + nkb, nkb)
        blocks.append((d <= 1, pl.multiple_of(jb * KB_D, KB_D), jnp.clip(jb - t, -2, 2) + 2))

    nblk = len(blocks)
    vmax = lambda s: jnp.max(jnp.max(s.reshape(KB_D // 8, 8, tq), axis=0), axis=0, keepdims=True)

    def scores(h, pos, st):
        near, k0, idx = blocks[pos]
        kb = k_ref[0, pl.ds(k0, KB_D), :]
        for i in range(2):
            slot = 2 * (h % 2) + i
            s = jnp.dot(kb, st["q"][i], preferred_element_type=F32)
            if near:
                s = s + tbl_ref[h, idx]
                s_scr[slot, pos] = s
                st["m"][i] = jnp.maximum(st["m"][i], vmax(s))
            else:
                s_scr[slot, pos] = s
                st["m"][i] = jnp.maximum(st["m"][i], vmax(s) + tbl_ref[h, idx, 0:1, :])

    def probs(h, pos, st):
        near, k0, idx = blocks[pos]
        for i in range(2):
            slot = 2 * (h % 2) + i
            shift = st["m"][i] if near else st["m"][i] - tbl_ref[h, idx, 0:1, :]
            p_scr[slot, pl.ds(k0, KB_D), :] = jnp.exp2(s_scr[slot, pos] - shift).astype(BF16)

    def new_state(h):
        return {"q": [jnp.where(rowgrp == 2 * h + i, qT, jnp.zeros_like(qT)) for i in range(2)],
                "m": [jnp.full((1, tq), -jnp.inf, F32)] * 2}

    ones = jnp.ones((16, k_ref.shape[1]), BF16)

    def finish(h, st):
        vh = jnp.concatenate([v_ref[0, h * HEAD_DIM:(h + 1) * HEAD_DIM, :], ones], axis=0)
        o = []
        for i in range(2):
            ol = jnp.dot(vh, p_scr[2 * (h % 2) + i], preferred_element_type=F32)
            o.append(ol[:HEAD_DIM] * (1.0 / ol[HEAD_DIM:HEAD_DIM + 1]))
        oh = o[0] - lam * o[1]
        ms = jnp.mean(oh * oh, axis=0, keepdims=True)
        return oh * lax.rsqrt(ms + RMS_EPS) * sub_ref[...] * (1.0 - lambda_init)

    def exact_max_path():
        st = new_state(0)
        for pos in range(nblk):
            scores(0, pos, st)
        parts = []
        for h in range(DIFF_HEADS):
            nxt = new_state(h + 1) if h + 1 < DIFF_HEADS else None
            for pos in range(nblk):
                if nxt is not None:
                    scores(h + 1, pos, nxt)
                probs(h, pos, st)
            parts.append(finish(h, st))
            st = nxt
        o_ref[0] = jnp.concatenate(parts, axis=0).T.astype(BF16)

    def bounded_path():
        bound = bound_ref[0]

        def block(h, st, pos):
            near, k0, idx = blocks[pos]
            kb = k_ref[0, pl.ds(k0, KB_D), :]
            for i in range(2):
                s = jnp.dot(kb, st["q"][i], preferred_element_type=F32)
                if near:
                    p = jnp.exp2(s + tbl_ref[h, idx] - bound)
                else:
                    p = jnp.exp2(s - (bound - tbl_ref[h, idx, 0:1, :]))
                p_scr[2 * (h % 2) + i, pl.ds(k0, KB_D), :] = p.astype(BF16)

        lag = 3
        parts = []
        states = [new_state(h) for h in range(DIFF_HEADS)]
        for h in range(DIFF_HEADS):
            for pos in range(nblk):
                block(h, states[h], pos)
                if h > 0 and pos == lag - 1:
                    parts.append(finish(h - 1, states[h - 1]))
        parts.append(finish(DIFF_HEADS - 1, states[-1]))
        o_ref[0] = jnp.concatenate(parts, axis=0).T.astype(BF16)

    pl.when(bound_ref[0] <= D_SAFE_BOUND)(bounded_path)
    pl.when(bound_ref[0] > D_SAFE_BOUND)(exact_max_path)


def _d_score_bound(qkn_d_l, t5_table):
    gq = jnp.max(jnp.abs(qkn_d_l[0])) * (DIFF_HALF ** -0.5 * LOG2E)
    gk = jnp.max(jnp.abs(qkn_d_l[1]))
    bias = jnp.max(jnp.abs(t5_table[:, T5_D0:T5_D0 + DIFF_HEADS])) * LOG2E
    return (1.02 * DIFF_HALF * gq * gk + bias).astype(F32).reshape(1)


def _d_call(bound, lam_p, subln, qT, k, vT, tbl, lambda_init):
    bt, _, n = qT.shape
    tq = TQ_D
    return pl.pallas_call(
        functools.partial(_d_kernel, lambda_init=lambda_init),
        grid=(bt, n // tq),
        in_specs=[
            pl.BlockSpec(memory_space=pltpu.SMEM),
            _const_spec((4, DIFF_HALF)),
            _const_spec((HEAD_DIM, 1)),
            pl.BlockSpec((1, 256, tq), lambda b, t: (b, 2, t)),
            pl.BlockSpec((1, n, 256), lambda b, t: (b, 0, 0)),
            pl.BlockSpec((1, 256, n), lambda b, t: (b, 0, 0)),
            _const_spec((DIFF_HEADS, 5, KB_D, tq)),
        ],
        out_specs=pl.BlockSpec((1, tq, 256), lambda b, t: (b, t, 0)),
        out_shape=jax.ShapeDtypeStruct((bt, n, 256), BF16),
        scratch_shapes=[
            pltpu.VMEM((4, n // KB_D, KB_D, tq), F32),
            pltpu.VMEM((4, n, tq), BF16),
        ],
        compiler_params=_cparams(("parallel", "arbitrary")),
        name="mixer_d",
    )(bound, lam_p, subln, qT, k, vT, tbl)


def _b_kernel(sink_ref, q_ref, k_ref, v_ref, bias_ref, o_ref):
    n = k_ref.shape[1]
    tq = TQ_B
    nt = n // tq
    grp = SW_HEADS // SW_KV_HEADS
    zero = jnp.zeros((HEAD_DIM, tq), BF16)
    sinks = [sink_ref[h] * LOG2E for h in range(SW_HEADS)]
    group = 2
    for t0 in range(0, nt, group):
        chains = []
        for t in range(t0, t0 + group):
            qT = q_ref[0, :, t * tq:(t + 1) * tq]
            qcat = []
            for h in range(SW_HEADS):
                qh = qT[h * HEAD_DIM:(h + 1) * HEAD_DIM]
                qcat.append(jnp.concatenate([qh, zero] if h // grp == 0 else [zero, qh], axis=0))
            qcat = jnp.concatenate(qcat, axis=1)
            blocks = [j for j in (-1, 0, 1) if 0 <= t + j < nt]
            s_all = [jnp.dot(k_ref[0, (t + j) * tq:(t + j + 1) * tq, :], qcat,
                             preferred_element_type=F32) for j in blocks]
            for h in range(SW_HEADS):
                chains.append((t, h, [(s[:, h * tq:(h + 1) * tq] + bias_ref[h, j + 1], t + j)
                                      for s, j in zip(s_all, blocks)]))
        m_l = []
        for t, h, ss in chains:
            m = jnp.full((1, tq), sinks[h], F32)
            for s, _ in ss:
                m = jnp.maximum(m, jnp.max(s, axis=0, keepdims=True))
            m_l.append(m)
        p_l = [[jnp.exp2(s - m) for s, _ in ss] for (t, h, ss), m in zip(chains, m_l)]
        outs = []
        for (t, h, ss), m, ps in zip(chains, m_l, p_l):
            kv = h // grp
            l = jnp.exp2(sinks[h] - m)
            o = jnp.zeros((HEAD_DIM, tq), F32)
            for p, (_, kb) in zip(ps, ss):
                l = l + jnp.sum(p, axis=0, keepdims=True)
                vh = v_ref[0, kv * HEAD_DIM:(kv + 1) * HEAD_DIM, kb * tq:(kb + 1) * tq]
                o = o + jnp.dot(vh, p.astype(BF16), preferred_element_type=F32)
            outs.append(o * (1.0 / l))
        for i, t in enumerate(range(t0, t0 + group)):
            o_ref[0, t * tq:(t + 1) * tq, :] = jnp.concatenate(
                outs[i * SW_HEADS:(i + 1) * SW_HEADS], axis=0).T.astype(BF16)


def _b_call(sink, qT, k, vT, bias):
    bt, _, n = qT.shape
    return pl.pallas_call(
        _b_kernel,
        grid=(bt,),
        in_specs=[
            pl.BlockSpec(memory_space=pltpu.SMEM),
            pl.BlockSpec((1, 256, n), lambda b: (b, 1, 0)),
            pl.BlockSpec((1, n, 128), lambda b: (b, 0, 4)),
            pl.BlockSpec((1, 128, n), lambda b: (b, 4, 0)),
            _const_spec((SW_HEADS, 3, TQ_B, TQ_B)),
        ],
        out_specs=pl.BlockSpec((1, n, 256), lambda b: (b, 0, 0)),
        out_shape=jax.ShapeDtypeStruct((bt, n, 256), BF16),
        compiler_params=_cparams(("parallel",)),
        name="mixer_b",
    )(sink, qT, k, vT, bias)


def _a_kernel(q_ref, k_ref, v_ref, bias_ref, o_ref):
    n = k_ref.shape[1]
    rows = n // GRID_W
    npair = n // A_PAIR
    kh = min(NA_WIN_H, rows)
    rowhead = lax.broadcasted_iota(jnp.int32, (NA_HEADS * HEAD_DIM, A_PAIR), 0) // HEAD_DIM
    group = 2
    for g0 in range(0, npair, group):
        geo, s_all = [], []
        for rp in range(g0, g0 + group):
            r = 2 * rp
            ws = min(max(r - kh // 2, 0), rows - kh, rows - A_WIN // GRID_W)
            var = rp if rp < 2 else (2 if rp < npair - 2 else rp - (npair - 5))
            q0, k0 = rp * A_PAIR, ws * GRID_W
            qT = q_ref[0, :, q0:q0 + A_PAIR]
            qcat = jnp.concatenate([jnp.where(rowhead == h, qT, jnp.zeros_like(qT))
                                    for h in range(NA_HEADS)], axis=1)
            s_all.append(jnp.dot(k_ref[0, k0:k0 + A_WIN, :], qcat, preferred_element_type=F32))
            geo.append((q0, k0, var))
        chains = [(i, h) for i in range(group) for h in range(NA_HEADS)]
        s_l = [s_all[i][:, h * A_PAIR:(h + 1) * A_PAIR] + bias_ref[h, geo[i][2]] for i, h in chains]
        m_l = [jnp.max(s, axis=0, keepdims=True) for s in s_l]
        p_l = [jnp.exp2(s - m) for s, m in zip(s_l, m_l)]
        l_l = [jnp.sum(p, axis=0, keepdims=True) for p in p_l]
        o_l = [jnp.dot(v_ref[0, h * HEAD_DIM:(h + 1) * HEAD_DIM, geo[i][1]:geo[i][1] + A_WIN],
                       p.astype(BF16), preferred_element_type=F32) * (1.0 / l)
               for (i, h), p, l in zip(chains, p_l, l_l)]
        for i in range(group):
            q0 = geo[i][0]
            o_ref[0, q0:q0 + A_PAIR, :] = jnp.concatenate(
                o_l[i * NA_HEADS:(i + 1) * NA_HEADS], axis=0).T.astype(BF16)


def _a_call(qT, k, vT, bias, layer):
    bt, _, n = qT.shape
    return pl.pallas_call(
        _a_kernel,
        grid=(bt,),
        in_specs=[
            pl.BlockSpec((1, 256, n), lambda b: (b, 0, 0)),
            pl.BlockSpec((1, n, 256), lambda b: (b, 0, 1)),
            pl.BlockSpec((1, 256, n), lambda b: (b, 1, 0)),
            pl.BlockSpec((NA_HEADS, 5, A_WIN, A_PAIR), lambda b: (layer, 0, 0, 0),
                         pipeline_mode=pl.Buffered(1)),
        ],
        out_specs=pl.BlockSpec((1, n, 256), lambda b: (b, 0, 0)),
        out_shape=jax.ShapeDtypeStruct((bt, n, 256), BF16),
        compiler_params=_cparams(("parallel",)),
        name="mixer_a",
    )(qT, k, vT, bias)


def _c_kernel(q_ref, k_ref, v_ref, bias_ref, o_ref, lse_ref, sub_scr, o_scr, lse_scr, *, dil):
    n = q_ref.shape[1]
    sub_len = n // dil
    win = min(2 * TQ_C, sub_len)
    nqb = sub_len // TQ_C
    lane_head = lax.broadcasted_iota(jnp.int32, (TQ_C, 2 * HEAD_DIM), 1) // HEAD_DIM

    def gather(p, carry):
        src = pl.ds(p, sub_len, stride=dil)
        dst = pl.ds(pl.multiple_of(p * sub_len, sub_len), sub_len)
        for i, ref in enumerate((q_ref, k_ref, v_ref)):
            sub_scr[i, dst, :] = ref[0, src, :].astype(BF16)
        return carry

    if dil > 1:
        lax.fori_loop(0, dil, gather, 0)
    else:
        for i, ref in enumerate((q_ref, k_ref, v_ref)):
            sub_scr[i] = ref[0].astype(BF16)

    group = 4
    for g0 in range(0, n // TQ_C, group):
        geo = []
        for item in range(g0, g0 + group):
            p, qb = divmod(item, nqb)
            q0 = p * sub_len + qb * TQ_C
            start = p * sub_len + min(max(qb * TQ_C - TQ_C // 2, 0), sub_len - win)
            var = 0 if qb == 0 else (2 if qb == nqb - 1 else 1)
            geo.extend((q0, start, var, j) for j in range(2))
        s_l = []
        for q0, start, var, j in geo:
            qblk = sub_scr[0, q0:q0 + TQ_C, :]
            qm = jnp.where(lane_head == j, qblk, jnp.zeros_like(qblk))
            s = lax.dot_general(qm, sub_scr[1, start:start + win, :], (((1,), (1,)), ((), ())),
                                preferred_element_type=F32)
            s_l.append(s + bias_ref[j, var])
        m_l = [jnp.max(s, axis=-1, keepdims=True) for s in s_l]
        p_l = [jnp.exp2(s - m) for s, m in zip(s_l, m_l)]
        l_l = [jnp.sum(pr, axis=-1, keepdims=True) for pr in p_l]
        o_l = [jnp.dot(pr.astype(BF16), sub_scr[2, start:start + win, :], preferred_element_type=F32)
               for pr, (q0, start, var, j) in zip(p_l, geo)]
        for i in range(0, len(geo), 2):
            q0 = geo[i][0]
            o0, o1 = o_l[i] * (1.0 / l_l[i]), o_l[i + 1] * (1.0 / l_l[i + 1])
            e0 = jnp.broadcast_to(m_l[i] + jnp.log2(l_l[i]), (TQ_C, 2 * HEAD_DIM))
            e1 = jnp.broadcast_to(m_l[i + 1] + jnp.log2(l_l[i + 1]), (TQ_C, 2 * HEAD_DIM))
            o_scr[q0:q0 + TQ_C, :] = jnp.where(lane_head == 0, o0, o1)
            lse_scr[q0:q0 + TQ_C, :] = jnp.where(lane_head == 0, e0, e1)

    def scatter(p, carry):
        dst = pl.ds(p, sub_len, stride=dil)
        src = pl.ds(pl.multiple_of(p * sub_len, sub_len), sub_len)
        o_ref[0, dst, :] = o_scr[src, :]
        lse_ref[0, dst, :] = lse_scr[src, :]
        return carry

    if dil > 1:
        lax.fori_loop(0, dil, scatter, 0)
    else:
        o_ref[0] = o_scr[...]
        lse_ref[0] = lse_scr[...]


def _c_call(qkv, bias, g, dil):
    bt, n, _ = qkv.shape
    sub_len = n // dil
    win = min(2 * TQ_C, sub_len)
    lanes = 2 * HEAD_DIM
    return pl.pallas_call(
        functools.partial(_c_kernel, dil=dil),
        grid=(bt,),
        in_specs=[
            pl.BlockSpec((1, n, lanes), lambda b: (b, 0, g)),
            pl.BlockSpec((1, n, lanes), lambda b: (b, 0, 3 + g)),
            pl.BlockSpec((1, n, lanes), lambda b: (b, 0, 6 + g)),
            _const_spec((2, 3, TQ_C, win)),
        ],
        out_specs=[
            pl.BlockSpec((1, n, lanes), lambda b: (b, 0, 0)),
            pl.BlockSpec((1, n, lanes), lambda b: (b, 0, 0)),
        ],
        out_shape=[
            jax.ShapeDtypeStruct((bt, n, lanes), F32),
            jax.ShapeDtypeStruct((bt, n, lanes), F32),
        ],
        scratch_shapes=[
            pltpu.VMEM((3, n, lanes), BF16),
            pltpu.VMEM((n, lanes), F32),
            pltpu.VMEM((n, lanes), F32),
        ],
        compiler_params=_cparams(("parallel",)),
        name=f"mixer_c{g}",
    )(qkv, qkv, qkv, bias)


def _out_kernel(*refs, nsrc, nb_first):
    x_refs, (mod_ref, oa_ref, ob_ref, od_ref, oc0_ref, oc1_ref, oc2_ref,
             l0_ref, l1_ref, l2_ref, w_ref, o_ref) = refs[:nsrc], refs[nsrc:]
    l0, l1, l2 = l0_ref[0], l1_ref[0], l2_ref[0]
    mx = jnp.maximum(jnp.maximum(l0, l1), l2)
    e0, e1, e2 = jnp.exp2(l0 - mx), jnp.exp2(l1 - mx), jnp.exp2(l2 - mx)
    inv = 1.0 / (e0 + e1 + e2)
    y = jnp.dot(oa_ref[0], w_ref[0:256, :], preferred_element_type=F32)
    y += jnp.dot(ob_ref[0], w_ref[256:512, :], preferred_element_type=F32)
    y += jnp.dot((oc0_ref[0] * (e0 * inv)).astype(BF16), w_ref[512:640, :], preferred_element_type=F32)
    y += jnp.dot((oc1_ref[0] * (e1 * inv)).astype(BF16), w_ref[640:768, :], preferred_element_type=F32)
    y += jnp.dot((oc2_ref[0] * (e2 * inv)).astype(BF16), w_ref[768:896, :], preferred_element_type=F32)
    y += jnp.dot(od_ref[0], w_ref[896:1152, :], preferred_element_type=F32)
    o_ref[0] = _pick_group(x_refs, nb_first) + mod_ref[0, 2:3, :] * y


def _out_call(xs, mod, o_a, o_b, o_d, o_c, lse_c, w_out):
    bt = sum(x.shape[0] for x in xs)
    _, n, d = xs[0].shape
    tn = TN_OUT
    tok = lambda w: pl.BlockSpec((1, tn, w), lambda b, t: (b, t, 0))
    return pl.pallas_call(
        functools.partial(_out_kernel, nsrc=len(xs), nb_first=xs[0].shape[0]),
        grid=(bt, n // tn),
        in_specs=_group_specs(xs, tn) + [pl.BlockSpec((1, 6, d), lambda b, t: (b, 0, 0)),
                  tok(256), tok(256), tok(256), tok(128), tok(128), tok(128),
                  tok(128), tok(128), tok(128), _const_spec(w_out.shape)],
        out_specs=tok(d),
        out_shape=jax.ShapeDtypeStruct((bt, n, d), F32),
        compiler_params=_cparams(("arbitrary", "arbitrary")),
        name="out_proj",
    )(*xs, mod, o_a, o_b, o_d, *o_c, *lse_c, w_out)


def _ffn_kernel(xp_ref, x_ref, xn_ref, mod_ref, g_ref, wup_ref, cw_ref, wdn_ref, *refs, nb_first):
    o_refs, (h_scr, acc_scr) = refs[:-2], refs[-2:]
    t = pl.program_id(1)
    nt = pl.num_programs(1)
    tn = x_ref.shape[1]
    ext = tn + 2 * HALO
    xe = jnp.concatenate([xp_ref[0], x_ref[0], xn_ref[0]], axis=0)
    h = _mod_norm(xe, g_ref[...], mod_ref[0, 4:5, :], mod_ref[0, 3:4, :])
    row = lax.broadcasted_iota(jnp.int32, (ext, 1), 0)
    keep = jnp.logical_and(jnp.logical_or(row >= HALO, t > 0),
                           jnp.logical_or(row < HALO + tn, t < nt - 1))
    h_scr[...] = jnp.where(keep, h, 0.0).astype(BF16)
    acc_scr[...] = jnp.zeros_like(acc_scr)

    nsplit = 4
    rows = tn // nsplit

    def conv(u, cw, r0):
        ue = u[r0:r0 + rows + 2 * HALO]
        prev = pltpu.roll(ue, 1, 0)[HALO:HALO + rows]
        nxt = pltpu.roll(ue, rows + 2 * HALO - 1, 0)[HALO:HALO + rows]
        return prev * cw[0:1] + ue[HALO:HALO + rows] * cw[1:2] + nxt * cw[2:3] + cw[3:4]

    hb = h_scr[...]
    up = lambda c: jnp.dot(hb, wup_ref[c], preferred_element_type=F32)
    u_val, u_gate = up(0), up(N_FF_CHUNKS)
    for c in range(N_FF_CHUNKS):
        nxt_u = []
        for part in range(nsplit):
            r0 = part * rows
            if c + 1 < N_FF_CHUNKS and part % (nsplit // 2) == 0:
                nxt_u.append(up(c + 1 + len(nxt_u) * N_FF_CHUNKS))
            val = conv(u_val, cw_ref[c], r0)
            gate = conv(u_gate, cw_ref[N_FF_CHUNKS + c], r0)
            act = (val * gate * (1.0 / (1.0 + jnp.exp(-gate)))).astype(BF16)
            acc_scr[r0:r0 + rows, :] += jnp.dot(act, wdn_ref[c], preferred_element_type=F32)
        if nxt_u:
            u_val, u_gate = nxt_u
    if len(o_refs) == 1:
        o_refs[0][0] = x_ref[0] + mod_ref[0, 5:6, :] * acc_scr[...]
    else:
        first = pl.program_id(0) < nb_first

        @pl.when(first)
        def _():
            o_refs[0][0] = x_ref[0] + mod_ref[0, 5:6, :] * acc_scr[...]

        @pl.when(jnp.logical_not(first))
        def _():
            o_refs[1][0] = x_ref[0] + mod_ref[0, 5:6, :] * acc_scr[...]


def _ffn_call(x, mod, g, wup, cw, wdn, nb_first=None):
    bt, n, d = x.shape
    tn = TN_FFN
    nb = tn // HALO
    nhalo = n // HALO
    if nb_first is None:
        out_specs = pl.BlockSpec((1, tn, d), lambda b, t: (b, t, 0))
        out_shape = jax.ShapeDtypeStruct((bt, n, d), F32)
    else:
        last = n // tn - 1
        out_specs = [
            pl.BlockSpec((1, tn, d), lambda b, t: (jnp.minimum(b, nb_first - 1),
                                                   jnp.where(b < nb_first, t, last), 0)),
            pl.BlockSpec((1, tn, d), lambda b, t: (jnp.maximum(b - nb_first, 0),
                                                   jnp.where(b < nb_first, 0, t), 0))]
        out_shape = [jax.ShapeDtypeStruct((nb_first, n, d), F32),
                     jax.ShapeDtypeStruct((bt - nb_first, n, d), F32)]
    return pl.pallas_call(
        functools.partial(_ffn_kernel, nb_first=nb_first),
        grid=(bt, n // tn),
        in_specs=[
            pl.BlockSpec((1, HALO, d), lambda b, t: (b, jnp.maximum(t * nb - 1, 0), 0)),
            pl.BlockSpec((1, tn, d), lambda b, t: (b, t, 0)),
            pl.BlockSpec((1, HALO, d), lambda b, t: (b, jnp.minimum((t + 1) * nb, nhalo - 1), 0)),
            pl.BlockSpec((1, 6, d), lambda b, t: (b, 0, 0)),
            _const_spec((1, d)),
            _const_spec(wup.shape),
            _const_spec(cw.shape),
            _const_spec(wdn.shape),
        ],
        out_specs=out_specs,
        out_shape=out_shape,
        scratch_shapes=[pltpu.VMEM((tn + 2 * HALO, d), BF16), pltpu.VMEM((tn, d), F32)],
        compiler_params=_cparams(("arbitrary", "arbitrary")),
        name="ffn",
    )(x, x, x, mod, g, wup, cw, wdn)


def _t5_bucket(rel):
    nb = T5_BUCKETS // 2
    max_exact = nb // 2
    base = jnp.where(rel > 0, nb, 0)
    n = jnp.abs(rel)
    nf = jnp.maximum(n, 1).astype(jnp.float32)
    large = max_exact + (jnp.log(nf / max_exact) / math.log(T5_MAX_DIST / max_exact)
                         * (nb - max_exact)).astype(jnp.int32)
    large = jnp.minimum(large, nb - 1)
    return base + jnp.where(n < max_exact, n, large)


def _t5_kernel(tab_ref, bkt_ref, o_ref, *, head0):
    h = head0 + pl.program_id(0)
    for v in range(bkt_ref.shape[0]):
        bkt = bkt_ref[v]
        acc = jnp.full(bkt.shape, NEG_INF, F32)
        for b in range(T5_BUCKETS):
            acc = jnp.where(bkt == b, tab_ref[b, h] * LOG2E, acc)
        o_ref[0, v] = acc


def _t5_bias(table, rel, mask, head0, nheads):
    bkt = jnp.where(jnp.asarray(mask), _t5_bucket(jnp.asarray(rel, dtype=jnp.int32)), -1)
    return pl.pallas_call(
        functools.partial(_t5_kernel, head0=head0),
        grid=(nheads,),
        in_specs=[pl.BlockSpec(memory_space=pltpu.SMEM), _const_spec(bkt.shape)],
        out_specs=pl.BlockSpec((1,) + bkt.shape, lambda h: (h, 0, 0, 0)),
        out_shape=jax.ShapeDtypeStruct((nheads,) + bkt.shape, F32),
        compiler_params=_cparams(("arbitrary",)),
        name="t5_bias",
    )(table.astype(F32), bkt)


def _bias_tables_t5(t5_table):
    kk = np.arange(KB_D)[:, None]
    qq = np.arange(TQ_D)[None, :]
    far = 8 * T5_MAX_DIST
    rel_d = np.stack([np.full((KB_D, TQ_D), -far)] + [j * KB_D + kk - qq for j in (-1, 0, 1)]
                     + [np.full((KB_D, TQ_D), far)])
    tbl_d = _t5_bias(t5_table, rel_d, np.ones_like(rel_d, bool), T5_D0, DIFF_HEADS)
    kk = np.arange(TQ_B)[:, None]
    qq = np.arange(TQ_B)[None, :]
    rel_b = np.stack([j * TQ_B + kk - qq for j in (-1, 0, 1)])
    tbl_b = _t5_bias(t5_table, rel_b, np.abs(rel_b) <= SW_HALF_WINDOW, T5_B0, SW_HEADS)
    tbl_c = []
    for g, (w, r) in enumerate(DIL_PAIRS):
        sub_len = SEQ // r
        hw = w // (2 * r)
        win = min(2 * TQ_C, sub_len)
        qq = np.arange(TQ_C)[:, None]
        kk = np.arange(win)[None, :]
        offs = (0, -(TQ_C // 2), -(win - TQ_C))
        rel_c = np.stack([o + kk - qq for o in offs])
        tbl_c.append(_t5_bias(t5_table, rel_c * r, np.abs(rel_c) <= hw, T5_C0 + 2 * g, 2))
    return tbl_d, tbl_b, tbl_c


def _a_variants():
    rows = SEQ // GRID_W
    kh = min(NA_WIN_H, rows)
    wrows = A_WIN // GRID_W
    out = []
    for r in (0, 2, 4, rows - 4, rows - 2):
        ws = min(int(np.clip(r - kh // 2, 0, rows - kh)), rows - wrows)
        per_row = []
        for kr in range(wrows):
            pair = []
            for a in range(2):
                rsq = int(np.clip(r + a - kh // 2, 0, rows - kh))
                krow = ws + kr
                pair.append(krow - (r + a) + NA_WIN_H - 1 if rsq <= krow < rsq + kh else None)
            per_row.append(tuple(pair))
        out.append(tuple(per_row))
    return tuple(out)


def _abias_kernel(rpb_ref, dcm_ref, o_ref):
    i = pl.program_id(0)
    ncol = 2 * NA_WIN_W - 1
    dcm = dcm_ref[...]
    second = lax.broadcasted_iota(jnp.int32, dcm.shape, 1) >= GRID_W
    neg = jnp.full(dcm.shape, NEG_INF, F32)
    for v, per_row in enumerate(_a_variants()):
        for kr, (dr0, dr1) in enumerate(per_row):
            if dr0 is None and dr1 is None:
                o_ref[0, v, kr * GRID_W:(kr + 1) * GRID_W, :] = neg
                continue

            def body(t, acc, dr0=dr0, dr1=dr1):
                s0 = NEG_INF if dr0 is None else rpb_ref[i, dr0 * ncol + t] * LOG2E
                s1 = NEG_INF if dr1 is None else rpb_ref[i, dr1 * ncol + t] * LOG2E
                return jnp.where(dcm == t, jnp.where(second, s1, s0), acc)

            o_ref[0, v, kr * GRID_W:(kr + 1) * GRID_W, :] = lax.fori_loop(0, ncol, body, neg)


def _bias_tables_a(rpb_a):
    nl, nh, nr, nc = rpb_a.shape
    kc = np.arange(GRID_W)[:, None]
    c = np.tile(np.arange(GRID_W), 2)[None, :]
    cs = np.clip(c - NA_WIN_W // 2, 0, GRID_W - NA_WIN_W)
    dcm = np.where((kc >= cs) & (kc < cs + NA_WIN_W), np.clip(kc - c + NA_WIN_W - 1, 0, nc - 1), -1)
    return pl.pallas_call(
        _abias_kernel,
        grid=(nl * nh,),
        in_specs=[pl.BlockSpec(memory_space=pltpu.SMEM), _const_spec(dcm.shape)],
        out_specs=pl.BlockSpec((1, 5, A_WIN, A_PAIR), lambda i: (i, 0, 0, 0)),
        out_shape=jax.ShapeDtypeStruct((nl * nh, 5, A_WIN, A_PAIR), F32),
        compiler_params=_cparams(("arbitrary",)),
        name="rpb_bias",
    )(rpb_a.reshape(nl * nh, nr * nc).astype(F32), jnp.asarray(dcm, jnp.int32))


def _in_weights(w_in_l, qkn_a, qkn_b, qkn_c, qkn_d):
    a0, b0, c0, d0 = 0, 768, 1280, 2432
    segs = ((a0, 256), (b0, 256), (d0, 256),
            (d0 + 512, 256), (a0 + 512, 256), (b0 + 384, 128),
            (d0 + 256, 256), (a0 + 256, 256), (b0 + 256, 128),
            (c0, 768))
    wt = jnp.concatenate([w_in_l[:, s0:s0 + w] for s0, w in segs], axis=1).T.astype(BF16)
    wvc = w_in_l[:, c0 + 768:c0 + 1152].astype(BF16)
    qs64 = HEAD_DIM ** -0.5 * LOG2E
    qs32 = DIFF_HALF ** -0.5 * LOG2E
    ones = jnp.ones((VT_ROWS,), F32)
    gs = jnp.concatenate([
        jnp.tile(qkn_a[0] * qs64, 4), jnp.tile(qkn_b[0] * qs64, 4), jnp.tile(qkn_d[0] * qs32, 8),
        ones,
        jnp.tile(qkn_d[1], 8), jnp.tile(qkn_a[1], 4), jnp.tile(qkn_b[1], 2),
        jnp.tile(qkn_c[0] * qs64, 6), jnp.tile(qkn_c[1], 6),
    ]).astype(F32).reshape(WT_ROWS, 1)
    return wt, wvc, gs


def _ffn_weights(w_up_l, conv_w_l, conv_b_l, w_down_l):
    d = w_up_l.shape[0]
    wup = w_up_l.reshape(d, 2 * N_FF_CHUNKS, FF_CHUNK).transpose(1, 0, 2).astype(BF16)
    cw = jnp.concatenate([conv_w_l, conv_b_l[None], jnp.zeros((4, 2 * D_FF), F32)], axis=0)
    cw = cw.reshape(8, 2 * N_FF_CHUNKS, FF_CHUNK).transpose(1, 0, 2)
    wdn = w_down_l.reshape(N_FF_CHUNKS, FF_CHUNK, d).astype(BF16)
    return wup, cw, wdn


def _layer(xs, mod, l, p, tables, split_out=None):
    tbl_a, tbl_d, tbl_b, tbl_c = tables
    wt, wvc, gs = _in_weights(p["w_in"][l], p["qkn_a"][l], p["qkn_b"][l], p["qkn_c"][l], p["qkn_d"][l])
    qT, vT, k, qkv_c = _in_call(xs, mod, p["norm_attn_g"][l].reshape(1, -1), wt, wvc, gs)
    lambda_init = 0.8 - 0.6 * math.exp(-0.3 * l)
    o_a = _a_call(qT, k, vT, tbl_a, l)
    o_b = _b_call(p["sink_b"][l], qT, k, vT, tbl_b)
    o_c, lse_c = [], []
    for g, (_, r) in enumerate(DIL_PAIRS):
        o, lse = _c_call(qkv_c, tbl_c[g], g, r)
        o_c.append(o)
        lse_c.append(lse)
    o_d = _d_call(_d_score_bound(p["qkn_d"][l], p["t5_table"]), p["lam_d"][l],
                  p["subln_d"][l].reshape(HEAD_DIM, 1), qT, k, vT, tbl_d, lambda_init)
    x = _out_call(xs, mod, o_a, o_b, o_d, o_c, lse_c, p["w_out"][l].astype(BF16))
    wup, cw, wdn = _ffn_weights(p["w_up"][l], p["conv_w"][l], p["conv_b"][l], p["w_down"][l])
    return _ffn_call(x, mod, p["norm_ffn_g"][l].reshape(1, -1), wup, cw, wdn, split_out)


def _trunk(xs, c, p, depth=DEPTH):
    bt, d = c.shape
    mod_all = _ada_call(c, p["w_ada"], p["b_ada"]).reshape(DEPTH, bt, 6, d)
    tables = (_bias_tables_a(p["rpb_a"]),) + _bias_tables_t5(p["t5_table"])
    split = xs[0].shape[0] if len(xs) == 2 else None
    x = list(xs)
    for l in range(depth):
        x = _layer(x, mod_all[l], l, p, tables, split if l == depth - 1 else None)
        x = list(x) if isinstance(x, (list, tuple)) else [x]
    return tuple(x)


def kernel(x_prompt, x_sample, c_prompt, c_sample, norm_attn_g, norm_ffn_g, w_ada, b_ada, w_in, qkn_a, qkn_b, qkn_c, qkn_d, rpb_a, sink_b, t5_table, lam_d, subln_d, w_out, w_up, conv_w, conv_b, w_down):
    p = dict(norm_attn_g=norm_attn_g, norm_ffn_g=norm_ffn_g, w_ada=w_ada, b_ada=b_ada, w_in=w_in,
             qkn_a=qkn_a, qkn_b=qkn_b, qkn_c=qkn_c, qkn_d=qkn_d, rpb_a=rpb_a, sink_b=sink_b,
             t5_table=t5_table, lam_d=lam_d, subln_d=subln_d, w_out=w_out, w_up=w_up,
             conv_w=conv_w, conv_b=conv_b, w_down=w_down)
    c = jnp.concatenate([c_prompt, c_sample], axis=0)
    return _trunk((x_prompt, x_sample), c, p)
```

```python
import functools
import math

import numpy as np
import jax
import jax.numpy as jnp
from jax import lax
from jax.experimental import pallas as pl
from jax.experimental.pallas import tpu as pltpu

D_MODEL = 1024
SEQ = 2048
DEPTH = 4
HEAD_DIM = 64
GRID_W = 64
NA_HEADS = 4
NA_WIN_H = 8
NA_WIN_W = 16
SW_HEADS = 4
SW_KV_HEADS = 2
SW_HALF_WINDOW = 128
DIL_PAIRS = ((128, 1), (512, 4), (2048, 16))
DIFF_HEADS = 4
DIFF_HALF = HEAD_DIM // 2
T5_BUCKETS = 32
T5_MAX_DIST = 128
T5_B0 = 0
T5_C0 = 4
T5_D0 = 10
D_FF = 2816
RMS_EPS = 1e-6
NEG_INF = -1e30
LOG2E = 1.4426950408889634

F32 = jnp.float32
BF16 = jnp.bfloat16

VMEM_LIMIT_BYTES = 56 * 1024 * 1024
TN_IN = 512
TN_OUT = 512
TN_FFN = 512
FF_CHUNK = 256
N_FF_CHUNKS = D_FF // FF_CHUNK
HALO = 8
TQ_D = 512
D_SAFE_BOUND = 60.0
KB_D = 256
D_NEAR = TQ_D // KB_D + 2
TQ_B = 128
TQ_C = 128
A_PAIR = 2 * GRID_W
A_WIN = 10 * GRID_W

QT_ROWS = 768
VT_ROWS = 640
K_ROWS = 640
C_QK_ROWS = 768
WT_ROWS = QT_ROWS + VT_ROWS + K_ROWS + C_QK_ROWS
C_WIDTH = 1152


def _cparams(sem):
    return pltpu.CompilerParams(dimension_semantics=sem, vmem_limit_bytes=VMEM_LIMIT_BYTES)


def _const_spec(shape):
    nd = len(shape)
    return pl.BlockSpec(shape, lambda *_: (0,) * nd, pipeline_mode=pl.Buffered(1))


def _ada_kernel(c_ref, w_ref, b_ref, o_ref):
    c = c_ref[...]
    sc = (c * (1.0 / (1.0 + jnp.exp(-c)))).astype(BF16)
    o_ref[0] = jnp.dot(sc, w_ref[0].astype(BF16), preferred_element_type=F32) + b_ref[0]


def _ada_call(c, w_ada, b_ada):
    bt = c.shape[0]
    ncol = 6 * D_MODEL
    cb = 1024
    return pl.pallas_call(
        _ada_kernel,
        grid=(DEPTH, ncol // cb),
        in_specs=[
            pl.BlockSpec((bt, D_MODEL), lambda l, j: (0, 0)),
            pl.BlockSpec((1, D_MODEL, cb), lambda l, j: (l, 0, j)),
            pl.BlockSpec((1, 1, cb), lambda l, j: (l, 0, j)),
        ],
        out_specs=pl.BlockSpec((1, bt, cb), lambda l, j: (l, 0, j)),
        out_shape=jax.ShapeDtypeStruct((DEPTH, bt, ncol), F32),
        compiler_params=_cparams(("arbitrary", "arbitrary")),
        name="ada_mod",
    )(c, w_ada, b_ada.reshape(DEPTH, 1, ncol))


def _mod_norm(x, g, scale, shift):
    ms = jnp.mean(x * x, axis=-1, keepdims=True)
    return x * lax.rsqrt(ms + RMS_EPS) * (g * (1.0 + scale)) + shift


def _pick_group(refs, nb_first):
    if len(refs) == 1:
        return refs[0][0]
    return jnp.where(pl.program_id(0) < nb_first, refs[0][0], refs[1][0])


def _group_specs(xs, tn):
    d = xs[0].shape[-1]
    if len(xs) == 1:
        return [pl.BlockSpec((1, tn, d), lambda b, t: (b, t, 0))]
    nb = xs[0].shape[0]
    last = xs[0].shape[1] // tn - 1
    return [pl.BlockSpec((1, tn, d), lambda b, t: (jnp.minimum(b, nb - 1), jnp.where(b < nb, t, last), 0)),
            pl.BlockSpec((1, tn, d), lambda b, t: (jnp.maximum(b - nb, 0), jnp.where(b < nb, 0, t), 0))]


def _in_kernel(*refs, nsrc, nb_first):
    x_refs, (mod_ref, g_ref, wt_ref, wvc_ref, gs_ref, qT_ref, vT_ref, k_ref, c_ref, h_scr) = \
        refs[:nsrc], refs[nsrc:]
    tn = x_refs[0].shape[1]
    nsub = 2
    ts = tn // nsub
    x = _pick_group(x_refs, nb_first)

    def norm_sub(i):
        h = _mod_norm(x[i * ts:(i + 1) * ts], g_ref[...], mod_ref[0, 1:2, :], mod_ref[0, 0:1, :])
        h_scr[i * ts:(i + 1) * ts, :] = h.astype(BF16)

    def head_norm(y, r0, grp):
        rows = y.shape[0]
        y3 = y.reshape(rows // grp, grp, ts)
        ms = jnp.sum(y3 * y3, axis=1, keepdims=True) * (1.0 / grp)
        gs = gs_ref[r0:r0 + rows, :].reshape(rows // grp, grp, 1)
        return (y3 * lax.rsqrt(ms + RMS_EPS) * gs).reshape(rows, ts)

    items = [(r0, 256, grp, ("qT", r0)) for r0, grp in ((0, HEAD_DIM), (256, HEAD_DIM), (512, DIFF_HALF))]
    items += [(QT_ROWS + r0, rows, None, ("vT", r0)) for r0, rows in ((0, 256), (256, 256), (512, 128))]
    base = QT_ROWS + VT_ROWS
    items += [(base + r0, rows, grp, ("k", r0))
              for r0, rows, grp in ((0, 256, DIFF_HALF), (256, 256, HEAD_DIM), (512, 128, HEAD_DIM))]
    base += K_ROWS
    items += [(base + r0, 256, HEAD_DIM, ("c", r0)) for r0 in (0, 256, 512)]
    items += [(None, C_WIDTH - C_QK_ROWS, None, ("c", C_QK_ROWS))]

    def matmul(item, i):
        w0, rows = item[0], item[1]
        hb = h_scr[i * ts:(i + 1) * ts, :]
        if w0 is None:
            return jnp.dot(hb, wvc_ref[...], preferred_element_type=F32)
        return lax.dot_general(wt_ref[w0:w0 + rows, :], hb, (((1,), (1,)), ((), ())),
                               preferred_element_type=F32)

    def finish(item, i, y):
        w0, rows, grp, (dst, r0) = item
        tok = slice(i * ts, (i + 1) * ts)
        if grp is not None:
            y = head_norm(y, w0, grp)
        if dst == "qT":
            qT_ref[0, r0:r0 + rows, tok] = y.astype(BF16)
        elif dst == "vT":
            vT_ref[0, r0:r0 + rows, tok] = y.astype(BF16)
        elif dst == "k":
            k_ref[0, tok, r0:r0 + rows] = y.T.astype(BF16)
        else:
            c_ref[0, tok, r0:r0 + rows] = y if w0 is None else y.T

    work = [(item, i) for i in range(nsub) for item in items]
    norm_sub(0)
    pending = None
    for n_issued, (item, i) in enumerate(work):
        if n_issued == 2:
            for j in range(1, nsub):
                norm_sub(j)
        y = matmul(item, i)
        if pending is not None:
            finish(*pending)
        pending = (item, i, y)
    finish(*pending)


def _in_call(xs, mod, g, wt, wvc, gs):
    bt = sum(x.shape[0] for x in xs)
    _, n, d = xs[0].shape
    tn = TN_IN
    return pl.pallas_call(
        functools.partial(_in_kernel, nsrc=len(xs), nb_first=xs[0].shape[0]),
        grid=(bt, n // tn),
        in_specs=_group_specs(xs, tn) + [
            pl.BlockSpec((1, 6, d), lambda b, t: (b, 0, 0)),
            _const_spec((1, d)),
            _const_spec((WT_ROWS, d)),
            _const_spec((d, C_WIDTH - C_QK_ROWS)),
            _const_spec((WT_ROWS, 1)),
        ],
        out_specs=[
            pl.BlockSpec((1, QT_ROWS, tn), lambda b, t: (b, 0, t)),
            pl.BlockSpec((1, VT_ROWS, tn), lambda b, t: (b, 0, t)),
            pl.BlockSpec((1, tn, K_ROWS), lambda b, t: (b, t, 0)),
            pl.BlockSpec((1, tn, C_WIDTH), lambda b, t: (b, t, 0)),
        ],
        out_shape=[
            jax.ShapeDtypeStruct((bt, QT_ROWS, n), BF16),
            jax.ShapeDtypeStruct((bt, VT_ROWS, n), BF16),
            jax.ShapeDtypeStruct((bt, n, K_ROWS), BF16),
            jax.ShapeDtypeStruct((bt, n, C_WIDTH), F32),
        ],
        scratch_shapes=[pltpu.VMEM((tn, d), BF16)],
        compiler_params=_cparams(("arbitrary", "arbitrary")),
        name="in_proj",
    )(*xs, mod, g, wt, wvc, gs)


def _d_kernel(bound_ref, lam_ref, sub_ref, q_ref, k_ref, v_ref, tbl_ref, o_ref, s_scr, p_scr, *, lambda_init):
    t = pl.program_id(1)
    tq = q_ref.shape[2]
    nkb = k_ref.shape[1] // KB_D
    qT = q_ref[0]
    rowgrp = lax.broadcasted_iota(jnp.int32, qT.shape, 0) // DIFF_HALF
    lp = lam_ref[...]
    lam = (jnp.exp(jnp.sum(lp[0:1] * lp[1:2], axis=-1, keepdims=True))
           - jnp.exp(jnp.sum(lp[2:3] * lp[3:4], axis=-1, keepdims=True)) + lambda_init)

    first = t * (tq // KB_D)
    blocks = []
    for d in range(-1, nkb - 1):
        jb = lax.rem(first + d + nkb, nkb)
        blocks.append((d < D_NEAR - 1, pl.multiple_of(jb * KB_D, KB_D),
                       jnp.clip(jb - first, -2, D_NEAR - 1) + 2))

    nblk = len(blocks)
    vmax = lambda s: jnp.max(jnp.max(s.reshape(KB_D // 8, 8, tq), axis=0), axis=0, keepdims=True)

    def scores(h, pos, st):
        near, k0, idx = blocks[pos]
        kb = k_ref[0, pl.ds(k0, KB_D), :]
        for i in range(2):
            slot = 2 * (h % 2) + i
            s = jnp.dot(kb, st["q"][i], preferred_element_type=F32)
            if near:
                s = s + tbl_ref[h, idx]
                s_scr[slot, pos] = s
                st["m"][i] = jnp.maximum(st["m"][i], vmax(s))
            else:
                s_scr[slot, pos] = s
                st["m"][i] = jnp.maximum(st["m"][i], vmax(s) + tbl_ref[h, idx, 0:1, :])

    def probs(h, pos, st):
        near, k0, idx = blocks[pos]
        for i in range(2):
            slot = 2 * (h % 2) + i
            shift = st["m"][i] if near else st["m"][i] - tbl_ref[h, idx, 0:1, :]
            p_scr[slot, pl.ds(k0, KB_D), :] = jnp.exp2(s_scr[slot, pos] - shift).astype(BF16)

    def new_state(h):
        return {"q": [jnp.where(rowgrp == 2 * h + i, qT, jnp.zeros_like(qT)) for i in range(2)],
                "m": [jnp.full((1, tq), -jnp.inf, F32)] * 2}

    ones = jnp.ones((16, k_ref.shape[1]), BF16)

    def finish(h, st):
        vh = jnp.concatenate([v_ref[0, h * HEAD_DIM:(h + 1) * HEAD_DIM, :], ones], axis=0)
        o = []
        for i in range(2):
            ol = jnp.dot(vh, p_scr[2 * (h % 2) + i], preferred_element_type=F32)
            o.append(ol[:HEAD_DIM] * (1.0 / ol[HEAD_DIM:HEAD_DIM + 1]))
        oh = o[0] - lam * o[1]
        ms = jnp.mean(oh * oh, axis=0, keepdims=True)
        return oh * lax.rsqrt(ms + RMS_EPS) * sub_ref[...] * (1.0 - lambda_init)

    def exact_max_path():
        st = new_state(0)
        for pos in range(nblk):
            scores(0, pos, st)
        parts = []
        for h in range(DIFF_HEADS):
            nxt = new_state(h + 1) if h + 1 < DIFF_HEADS else None
            for pos in range(nblk):
                if nxt is not None:
                    scores(h + 1, pos, nxt)
                probs(h, pos, st)
            parts.append(finish(h, st))
            st = nxt
        o_ref[0] = jnp.concatenate(parts, axis=0).T.astype(BF16)

    def bounded_path():
        bound = bound_ref[0]

        def block(h, st, pos):
            near, k0, idx = blocks[pos]
            kb = k_ref[0, pl.ds(k0, KB_D), :]
            for i in range(2):
                s = jnp.dot(kb, st["q"][i], preferred_element_type=F32)
                if near:
                    p = jnp.exp2(s + tbl_ref[h, idx] - bound)
                else:
                    p = jnp.exp2(s - (bound - tbl_ref[h, idx, 0:1, :]))
                p_scr[2 * (h % 2) + i, pl.ds(k0, KB_D), :] = p.astype(BF16)

        lag = 3
        parts = []
        states = [new_state(h) for h in range(DIFF_HEADS)]
        for h in range(DIFF_HEADS):
            for pos in range(nblk):
                block(h, states[h], pos)
                if h > 0 and pos == lag - 1:
                    parts.append(finish(h - 1, states[h - 1]))
        parts.append(finish(DIFF_HEADS - 1, states[-1]))
        o_ref[0] = jnp.concatenate(parts, axis=0).T.astype(BF16)

    pl.when(bound_ref[0] <= D_SAFE_BOUND)(bounded_path)
    pl.when(bound_ref[0] > D_SAFE_BOUND)(exact_max_path)


def _d_score_bound(qkn_d_l, t5_table):
    gq = jnp.max(jnp.abs(qkn_d_l[0])) * (DIFF_HALF ** -0.5 * LOG2E)
    gk = jnp.max(jnp.abs(qkn_d_l[1]))
    bias = jnp.max(jnp.abs(t5_table[:, T5_D0:T5_D0 + DIFF_HEADS])) * LOG2E
    return (1.02 * DIFF_HALF * gq * gk + bias).astype(F32).reshape(1)


def _d_call(bound, lam_p, subln, qT, k, vT, tbl, lambda_init):
    bt, _, n = qT.shape
    tq = TQ_D
    return pl.pallas_call(
        functools.partial(_d_kernel, lambda_init=lambda_init),
        grid=(bt, n // tq),
        in_specs=[
            pl.BlockSpec(memory_space=pltpu.SMEM),
            _const_spec((4, DIFF_HALF)),
            _const_spec((HEAD_DIM, 1)),
            pl.BlockSpec((1, 256, tq), lambda b, t: (b, 2, t)),
            pl.BlockSpec((1, n, 256), lambda b, t: (b, 0, 0)),
            pl.BlockSpec((1, 256, n), lambda b, t: (b, 0, 0)),
            _const_spec((DIFF_HEADS, D_NEAR + 2, KB_D, tq)),
        ],
        out_specs=pl.BlockSpec((1, tq, 256), lambda b, t: (b, t, 0)),
        out_shape=jax.ShapeDtypeStruct((bt, n, 256), BF16),
        scratch_shapes=[
            pltpu.VMEM((4, n // KB_D, KB_D, tq), F32),
            pltpu.VMEM((4, n, tq), BF16),
        ],
        compiler_params=_cparams(("parallel", "arbitrary")),
        name="mixer_d",
    )(bound, lam_p, subln, qT, k, vT, tbl)


def _b_kernel(sink_ref, q_ref, k_ref, v_ref, bias_ref, o_ref):
    n = k_ref.shape[1]
    tq = TQ_B
    nt = n // tq
    grp = SW_HEADS // SW_KV_HEADS
    zero = jnp.zeros((HEAD_DIM, tq), BF16)
    sinks = [sink_ref[h] * LOG2E for h in range(SW_HEADS)]
    group = 2
    for t0 in range(0, nt, group):
        chains = []
        for t in range(t0, t0 + group):
            qT = q_ref[0, :, t * tq:(t + 1) * tq]
            qcat = []
            for h in range(SW_HEADS):
                qh = qT[h * HEAD_DIM:(h + 1) * HEAD_DIM]
                qcat.append(jnp.concatenate([qh, zero] if h // grp == 0 else [zero, qh], axis=0))
            qcat = jnp.concatenate(qcat, axis=1)
            blocks = [j for j in (-1, 0, 1) if 0 <= t + j < nt]
            s_all = [jnp.dot(k_ref[0, (t + j) * tq:(t + j + 1) * tq, :], qcat,
                             preferred_element_type=F32) for j in blocks]
            for h in range(SW_HEADS):
                chains.append((t, h, [(s[:, h * tq:(h + 1) * tq] + bias_ref[h, j + 1], t + j)
                                      for s, j in zip(s_all, blocks)]))
        m_l = []
        for t, h, ss in chains:
            m = jnp.full((1, tq), sinks[h], F32)
            for s, _ in ss:
                m = jnp.maximum(m, jnp.max(s, axis=0, keepdims=True))
            m_l.append(m)
        p_l = [[jnp.exp2(s - m) for s, _ in ss] for (t, h, ss), m in zip(chains, m_l)]
        outs = []
        for (t, h, ss), m, ps in zip(chains, m_l, p_l):
            kv = h // grp
            l = jnp.exp2(sinks[h] - m)
            o = jnp.zeros((HEAD_DIM, tq), F32)
            for p, (_, kb) in zip(ps, ss):
                l = l + jnp.sum(p, axis=0, keepdims=True)
                vh = v_ref[0, kv * HEAD_DIM:(kv + 1) * HEAD_DIM, kb * tq:(kb + 1) * tq]
                o = o + jnp.dot(vh, p.astype(BF16), preferred_element_type=F32)
            outs.append(o * (1.0 / l))
        for i, t in enumerate(range(t0, t0 + group)):
            o_ref[0, t * tq:(t + 1) * tq, :] = jnp.concatenate(
                outs[i * SW_HEADS:(i + 1) * SW_HEADS], axis=0).T.astype(BF16)


def _b_call(sink, qT, k, vT, bias):
    bt, _, n = qT.shape
    return pl.pallas_call(
        _b_kernel,
        grid=(bt,),
        in_specs=[
            pl.BlockSpec(memory_space=pltpu.SMEM),
            pl.BlockSpec((1, 256, n), lambda b: (b, 1, 0)),
            pl.BlockSpec((1, n, 128), lambda b: (b, 0, 4)),
            pl.BlockSpec((1, 128, n), lambda b: (b, 4, 0)),
            _const_spec((SW_HEADS, 3, TQ_B, TQ_B)),
        ],
        out_specs=pl.BlockSpec((1, n, 256), lambda b: (b, 0, 0)),
        out_shape=jax.ShapeDtypeStruct((bt, n, 256), BF16),
        compiler_params=_cparams(("parallel",)),
        name="mixer_b",
    )(sink, qT, k, vT, bias)


def _a_kernel(q_ref, k_ref, v_ref, bias_ref, o_ref):
    n = k_ref.shape[1]
    rows = n // GRID_W
    npair = n // A_PAIR
    kh = min(NA_WIN_H, rows)
    rowhead = lax.broadcasted_iota(jnp.int32, (NA_HEADS * HEAD_DIM, A_PAIR), 0) // HEAD_DIM
    group = 2
    for g0 in range(0, npair, group):
        geo, s_all = [], []
        for rp in range(g0, g0 + group):
            r = 2 * rp
            ws = min(max(r - kh // 2, 0), rows - kh, rows - A_WIN // GRID_W)
            var = rp if rp < 2 else (2 if rp < npair - 2 else rp - (npair - 5))
            q0, k0 = rp * A_PAIR, ws * GRID_W
            qT = q_ref[0, :, q0:q0 + A_PAIR]
            qcat = jnp.concatenate([jnp.where(rowhead == h, qT, jnp.zeros_like(qT))
                                    for h in range(NA_HEADS)], axis=1)
            s_all.append(jnp.dot(k_ref[0, k0:k0 + A_WIN, :], qcat, preferred_element_type=F32))
            geo.append((q0, k0, var))
        chains = [(i, h) for i in range(group) for h in range(NA_HEADS)]
        s_l = [s_all[i][:, h * A_PAIR:(h + 1) * A_PAIR] + bias_ref[h, geo[i][2]] for i, h in chains]
        m_l = [jnp.max(s, axis=0, keepdims=True) for s in s_l]
        p_l = [jnp.exp2(s - m) for s, m in zip(s_l, m_l)]
        l_l = [jnp.sum(p, axis=0, keepdims=True) for p in p_l]
        o_l = [jnp.dot(v_ref[0, h * HEAD_DIM:(h + 1) * HEAD_DIM, geo[i][1]:geo[i][1] + A_WIN],
                       p.astype(BF16), preferred_element_type=F32) * (1.0 / l)
               for (i, h), p, l in zip(chains, p_l, l_l)]
        for i in range(group):
            q0 = geo[i][0]
            o_ref[0, q0:q0 + A_PAIR, :] = jnp.concatenate(
                o_l[i * NA_HEADS:(i + 1) * NA_HEADS], axis=0).T.astype(BF16)


def _a_call(qT, k, vT, bias, layer):
    bt, _, n = qT.shape
    return pl.pallas_call(
        _a_kernel,
        grid=(bt,),
        in_specs=[
            pl.BlockSpec((1, 256, n), lambda b: (b, 0, 0)),
            pl.BlockSpec((1, n, 256), lambda b: (b, 0, 1)),
            pl.BlockSpec((1, 256, n), lambda b: (b, 1, 0)),
            pl.BlockSpec((NA_HEADS, 5, A_WIN, A_PAIR), lambda b: (layer, 0, 0, 0),
                         pipeline_mode=pl.Buffered(1)),
        ],
        out_specs=pl.BlockSpec((1, n, 256), lambda b: (b, 0, 0)),
        out_shape=jax.ShapeDtypeStruct((bt, n, 256), BF16),
        compiler_params=_cparams(("parallel",)),
        name="mixer_a",
    )(qT, k, vT, bias)


def _c_kernel(q_ref, k_ref, v_ref, bias_ref, o_ref, lse_ref, sub_scr, o_scr, lse_scr, *, dil):
    n = q_ref.shape[1]
    sub_len = n // dil
    win = min(2 * TQ_C, sub_len)
    nqb = sub_len // TQ_C
    lane_head = lax.broadcasted_iota(jnp.int32, (TQ_C, 2 * HEAD_DIM), 1) // HEAD_DIM

    def gather(p, carry):
        src = pl.ds(p, sub_len, stride=dil)
        dst = pl.ds(pl.multiple_of(p * sub_len, sub_len), sub_len)
        for i, ref in enumerate((q_ref, k_ref, v_ref)):
            sub_scr[i, dst, :] = ref[0, src, :].astype(BF16)
        return carry

    if dil > 1:
        lax.fori_loop(0, dil, gather, 0)
    else:
        for i, ref in enumerate((q_ref, k_ref, v_ref)):
            sub_scr[i] = ref[0].astype(BF16)

    group = 4
    for g0 in range(0, n // TQ_C, group):
        geo = []
        for item in range(g0, g0 + group):
            p, qb = divmod(item, nqb)
            q0 = p * sub_len + qb * TQ_C
            start = p * sub_len + min(max(qb * TQ_C - TQ_C // 2, 0), sub_len - win)
            var = 0 if qb == 0 else (2 if qb == nqb - 1 else 1)
            geo.extend((q0, start, var, j) for j in range(2))
        s_l = []
        for q0, start, var, j in geo:
            qblk = sub_scr[0, q0:q0 + TQ_C, :]
            qm = jnp.where(lane_head == j, qblk, jnp.zeros_like(qblk))
            s = lax.dot_general(qm, sub_scr[1, start:start + win, :], (((1,), (1,)), ((), ())),
                                preferred_element_type=F32)
            s_l.append(s + bias_ref[j, var])
        m_l = [jnp.max(s, axis=-1, keepdims=True) for s in s_l]
        p_l = [jnp.exp2(s - m) for s, m in zip(s_l, m_l)]
        l_l = [jnp.sum(pr, axis=-1, keepdims=True) for pr in p_l]
        o_l = [jnp.dot(pr.astype(BF16), sub_scr[2, start:start + win, :], preferred_element_type=F32)
               for pr, (q0, start, var, j) in zip(p_l, geo)]
        for i in range(0, len(geo), 2):
            q0 = geo[i][0]
            o0, o1 = o_l[i] * (1.0 / l_l[i]), o_l[i + 1] * (1.0 / l_l[i + 1])
            e0 = jnp.broadcast_to(m_l[i] + jnp.log2(l_l[i]), (TQ_C, 2 * HEAD_DIM))
            e1 = jnp.broadcast_to(m_l[i + 1] + jnp.log2(l_l[i + 1]), (TQ_C, 2 * HEAD_DIM))
            o_scr[q0:q0 + TQ_C, :] = jnp.where(lane_head == 0, o0, o1)
            lse_scr[q0:q0 + TQ_C, :] = jnp.where(lane_head == 0, e0, e1)

    def scatter(p, carry):
        dst = pl.ds(p, sub_len, stride=dil)
        src = pl.ds(pl.multiple_of(p * sub_len, sub_len), sub_len)
        o_ref[0, dst, :] = o_scr[src, :]
        lse_ref[0, dst, :] = lse_scr[src, :]
        return carry

    if dil > 1:
        lax.fori_loop(0, dil, scatter, 0)
    else:
        o_ref[0] = o_scr[...]
        lse_ref[0] = lse_scr[...]


def _c_call(qkv, bias, g, dil):
    bt, n, _ = qkv.shape
    sub_len = n // dil
    win = min(2 * TQ_C, sub_len)
    lanes = 2 * HEAD_DIM
    return pl.pallas_call(
        functools.partial(_c_kernel, dil=dil),
        grid=(bt,),
        in_specs=[
            pl.BlockSpec((1, n, lanes), lambda b: (b, 0, g)),
            pl.BlockSpec((1, n, lanes), lambda b: (b, 0, 3 + g)),
            pl.BlockSpec((1, n, lanes), lambda b: (b, 0, 6 + g)),
            _const_spec((2, 3, TQ_C, win)),
        ],
        out_specs=[
            pl.BlockSpec((1, n, lanes), lambda b: (b, 0, 0)),
            pl.BlockSpec((1, n, lanes), lambda b: (b, 0, 0)),
        ],
        out_shape=[
            jax.ShapeDtypeStruct((bt, n, lanes), F32),
            jax.ShapeDtypeStruct((bt, n, lanes), F32),
        ],
        scratch_shapes=[
            pltpu.VMEM((3, n, lanes), BF16),
            pltpu.VMEM((n, lanes), F32),
            pltpu.VMEM((n, lanes), F32),
        ],
        compiler_params=_cparams(("parallel",)),
        name=f"mixer_c{g}",
    )(qkv, qkv, qkv, bias)


def _out_kernel(*refs, nsrc, nb_first):
    x_refs, (mod_ref, oa_ref, ob_ref, od_ref, oc0_ref, oc1_ref, oc2_ref,
             l0_ref, l1_ref, l2_ref, w_ref, o_ref) = refs[:nsrc], refs[nsrc:]
    l0, l1, l2 = l0_ref[0], l1_ref[0], l2_ref[0]
    mx = jnp.maximum(jnp.maximum(l0, l1), l2)
    e0, e1, e2 = jnp.exp2(l0 - mx), jnp.exp2(l1 - mx), jnp.exp2(l2 - mx)
    inv = 1.0 / (e0 + e1 + e2)
    y = jnp.dot(oa_ref[0], w_ref[0:256, :], preferred_element_type=F32)
    y += jnp.dot(ob_ref[0], w_ref[256:512, :], preferred_element_type=F32)
    y += jnp.dot((oc0_ref[0] * (e0 * inv)).astype(BF16), w_ref[512:640, :], preferred_element_type=F32)
    y += jnp.dot((oc1_ref[0] * (e1 * inv)).astype(BF16), w_ref[640:768, :], preferred_element_type=F32)
    y += jnp.dot((oc2_ref[0] * (e2 * inv)).astype(BF16), w_ref[768:896, :], preferred_element_type=F32)
    y += jnp.dot(od_ref[0], w_ref[896:1152, :], preferred_element_type=F32)
    o_ref[0] = _pick_group(x_refs, nb_first) + mod_ref[0, 2:3, :] * y


def _out_call(xs, mod, o_a, o_b, o_d, o_c, lse_c, w_out):
    bt = sum(x.shape[0] for x in xs)
    _, n, d = xs[0].shape
    tn = TN_OUT
    tok = lambda w: pl.BlockSpec((1, tn, w), lambda b, t: (b, t, 0))
    return pl.pallas_call(
        functools.partial(_out_kernel, nsrc=len(xs), nb_first=xs[0].shape[0]),
        grid=(bt, n // tn),
        in_specs=_group_specs(xs, tn) + [pl.BlockSpec((1, 6, d), lambda b, t: (b, 0, 0)),
                  tok(256), tok(256), tok(256), tok(128), tok(128), tok(128),
                  tok(128), tok(128), tok(128), _const_spec(w_out.shape)],
        out_specs=tok(d),
        out_shape=jax.ShapeDtypeStruct((bt, n, d), F32),
        compiler_params=_cparams(("arbitrary", "arbitrary")),
        name="out_proj",
    )(*xs, mod, o_a, o_b, o_d, *o_c, *lse_c, w_out)


def _ffn_kernel(xp_ref, x_ref, xn_ref, mod_ref, g_ref, wup_ref, cw_ref, wdn_ref, *refs, nb_first):
    o_refs, (h_scr, acc_scr) = refs[:-2], refs[-2:]
    t = pl.program_id(1)
    nt = pl.num_programs(1)
    tn = x_ref.shape[1]
    ext = tn + 2 * HALO
    xe = jnp.concatenate([xp_ref[0], x_ref[0], xn_ref[0]], axis=0)
    h = _mod_norm(xe, g_ref[...], mod_ref[0, 4:5, :], mod_ref[0, 3:4, :])
    row = lax.broadcasted_iota(jnp.int32, (ext, 1), 0)
    keep = jnp.logical_and(jnp.logical_or(row >= HALO, t > 0),
                           jnp.logical_or(row < HALO + tn, t < nt - 1))
    h_scr[...] = jnp.where(keep, h, 0.0).astype(BF16)
    acc_scr[...] = jnp.zeros_like(acc_scr)

    nsplit = 4
    rows = tn // nsplit

    def conv(u, cw, r0):
        ue = u[r0:r0 + rows + 2 * HALO]
        prev = pltpu.roll(ue, 1, 0)[HALO:HALO + rows]
        nxt = pltpu.roll(ue, rows + 2 * HALO - 1, 0)[HALO:HALO + rows]
        return prev * cw[0:1] + ue[HALO:HALO + rows] * cw[1:2] + nxt * cw[2:3] + cw[3:4]

    hb = h_scr[...]
    up = lambda c: jnp.dot(hb, wup_ref[c], preferred_element_type=F32)
    u_val, u_gate = up(0), up(N_FF_CHUNKS)
    for c in range(N_FF_CHUNKS):
        nxt_u = []
        for part in range(nsplit):
            r0 = part * rows
            if c + 1 < N_FF_CHUNKS and part % (nsplit // 2) == 0:
                nxt_u.append(up(c + 1 + len(nxt_u) * N_FF_CHUNKS))
            val = conv(u_val, cw_ref[c], r0)
            gate = conv(u_gate, cw_ref[N_FF_CHUNKS + c], r0)
            act = (val * gate * (1.0 / (1.0 + jnp.exp(-gate)))).astype(BF16)
            acc_scr[r0:r0 + rows, :] += jnp.dot(act, wdn_ref[c], preferred_element_type=F32)
        if nxt_u:
            u_val, u_gate = nxt_u
    if len(o_refs) == 1:
        o_refs[0][0] = x_ref[0] + mod_ref[0, 5:6, :] * acc_scr[...]
    else:
        first = pl.program_id(0) < nb_first

        @pl.when(first)
        def _():
            o_refs[0][0] = x_ref[0] + mod_ref[0, 5:6, :] * acc_scr[...]

        @pl.when(jnp.logical_not(first))
        def _():
            o_refs[1][0] = x_ref[0] + mod_ref[0, 5:6, :] * acc_scr[...]


def _ffn_call(x, mod, g, wup, cw, wdn, nb_first=None):
    bt, n, d = x.shape
    tn = TN_FFN
    nb = tn // HALO
    nhalo = n // HALO
    if nb_first is None:
        out_specs = pl.BlockSpec((1, tn, d), lambda b, t: (b, t, 0))
        out_shape = jax.ShapeDtypeStruct((bt, n, d), F32)
    else:
        last = n // tn - 1
        out_specs = [
            pl.BlockSpec((1, tn, d), lambda b, t: (jnp.minimum(b, nb_first - 1),
                                                   jnp.where(b < nb_first, t, last), 0)),
            pl.BlockSpec((1, tn, d), lambda b, t: (jnp.maximum(b - nb_first, 0),
                                                   jnp.where(b < nb_first, 0, t), 0))]
        out_shape = [jax.ShapeDtypeStruct((nb_first, n, d), F32),
                     jax.ShapeDtypeStruct((bt - nb_first, n, d), F32)]
    return pl.pallas_call(
        functools.partial(_ffn_kernel, nb_first=nb_first),
        grid=(bt, n // tn),
        in_specs=[
            pl.BlockSpec((1, HALO, d), lambda b, t: (b, jnp.maximum(t * nb - 1, 0), 0)),
            pl.BlockSpec((1, tn, d), lambda b, t: (b, t, 0)),
            pl.BlockSpec((1, HALO, d), lambda b, t: (b, jnp.minimum((t + 1) * nb, nhalo - 1), 0)),
            pl.BlockSpec((1, 6, d), lambda b, t: (b, 0, 0)),
            _const_spec((1, d)),
            _const_spec(wup.shape),
            _const_spec(cw.shape),
            _const_spec(wdn.shape),
        ],
        out_specs=out_specs,
        out_shape=out_shape,
        scratch_shapes=[pltpu.VMEM((tn + 2 * HALO, d), BF16), pltpu.VMEM((tn, d), F32)],
        compiler_params=_cparams(("arbitrary", "arbitrary")),
        name="ffn",
    )(x, x, x, mod, g, wup, cw, wdn)


def _t5_bucket(rel):
    nb = T5_BUCKETS // 2
    max_exact = nb // 2
    base = jnp.where(rel > 0, nb, 0)
    n = jnp.abs(rel)
    nf = jnp.maximum(n, 1).astype(jnp.float32)
    large = max_exact + (jnp.log(nf / max_exact) / math.log(T5_MAX_DIST / max_exact)
                         * (nb - max_exact)).astype(jnp.int32)
    large = jnp.minimum(large, nb - 1)
    return base + jnp.where(n < max_exact, n, large)


def _t5_kernel(tab_ref, bkt_ref, o_ref, *, head0):
    h = head0 + pl.program_id(0)
    for v in range(bkt_ref.shape[0]):
        bkt = bkt_ref[v]
        acc = jnp.full(bkt.shape, NEG_INF, F32)
        for b in range(T5_BUCKETS):
            acc = jnp.where(bkt == b, tab_ref[b, h] * LOG2E, acc)
        o_ref[0, v] = acc


def _t5_bias(table, rel, mask, head0, nheads):
    bkt = jnp.where(jnp.asarray(mask), _t5_bucket(jnp.asarray(rel, dtype=jnp.int32)), -1)
    return pl.pallas_call(
        functools.partial(_t5_kernel, head0=head0),
        grid=(nheads,),
        in_specs=[pl.BlockSpec(memory_space=pltpu.SMEM), _const_spec(bkt.shape)],
        out_specs=pl.BlockSpec((1,) + bkt.shape, lambda h: (h, 0, 0, 0)),
        out_shape=jax.ShapeDtypeStruct((nheads,) + bkt.shape, F32),
        compiler_params=_cparams(("arbitrary",)),
        name="t5_bias",
    )(table.astype(F32), bkt)


def _bias_tables_t5(t5_table):
    kk = np.arange(KB_D)[:, None]
    qq = np.arange(TQ_D)[None, :]
    far = 8 * T5_MAX_DIST
    rel_d = np.stack([np.full((KB_D, TQ_D), -far)]
                     + [j * KB_D + kk - qq for j in range(-1, D_NEAR - 1)]
                     + [np.full((KB_D, TQ_D), far)])
    tbl_d = _t5_bias(t5_table, rel_d, np.ones_like(rel_d, bool), T5_D0, DIFF_HEADS)
    kk = np.arange(TQ_B)[:, None]
    qq = np.arange(TQ_B)[None, :]
    rel_b = np.stack([j * TQ_B + kk - qq for j in (-1, 0, 1)])
    tbl_b = _t5_bias(t5_table, rel_b, np.abs(rel_b) <= SW_HALF_WINDOW, T5_B0, SW_HEADS)
    tbl_c = []
    for g, (w, r) in enumerate(DIL_PAIRS):
        sub_len = SEQ // r
        hw = w // (2 * r)
        win = min(2 * TQ_C, sub_len)
        qq = np.arange(TQ_C)[:, None]
        kk = np.arange(win)[None, :]
        offs = (0, -(TQ_C // 2), -(win - TQ_C))
        rel_c = np.stack([o + kk - qq for o in offs])
        tbl_c.append(_t5_bias(t5_table, rel_c * r, np.abs(rel_c) <= hw, T5_C0 + 2 * g, 2))
    return tbl_d, tbl_b, tbl_c


def _a_variants():
    rows = SEQ // GRID_W
    kh = min(NA_WIN_H, rows)
    wrows = A_WIN // GRID_W
    out = []
    for r in (0, 2, 4, rows - 4, rows - 2):
        ws = min(int(np.clip(r - kh // 2, 0, rows - kh)), rows - wrows)
        per_row = []
        for kr in range(wrows):
            pair = []
            for a in range(2):
                rsq = int(np.clip(r + a - kh // 2, 0, rows - kh))
                krow = ws + kr
                pair.append(krow - (r + a) + NA_WIN_H - 1 if rsq <= krow < rsq + kh else None)
            per_row.append(tuple(pair))
        out.append(tuple(per_row))
    return tuple(out)


def _abias_kernel(rpb_ref, dcm_ref, o_ref):
    i = pl.program_id(0)
    ncol = 2 * NA_WIN_W - 1
    dcm = dcm_ref[...]
    second = lax.broadcasted_iota(jnp.int32, dcm.shape, 1) >= GRID_W
    neg = jnp.full(dcm.shape, NEG_INF, F32)
    for v, per_row in enumerate(_a_variants()):
        for kr, (dr0, dr1) in enumerate(per_row):
            if dr0 is None and dr1 is None:
                o_ref[0, v, kr * GRID_W:(kr + 1) * GRID_W, :] = neg
                continue

            def body(t, acc, dr0=dr0, dr1=dr1):
                s0 = NEG_INF if dr0 is None else rpb_ref[i, dr0 * ncol + t] * LOG2E
                s1 = NEG_INF if dr1 is None else rpb_ref[i, dr1 * ncol + t] * LOG2E
                return jnp.where(dcm == t, jnp.where(second, s1, s0), acc)

            o_ref[0, v, kr * GRID_W:(kr + 1) * GRID_W, :] = lax.fori_loop(0, ncol, body, neg)


def _bias_tables_a(rpb_a):
    nl, nh, nr, nc = rpb_a.shape
    kc = np.arange(GRID_W)[:, None]
    c = np.tile(np.arange(GRID_W), 2)[None, :]
    cs = np.clip(c - NA_WIN_W // 2, 0, GRID_W - NA_WIN_W)
    dcm = np.where((kc >= cs) & (kc < cs + NA_WIN_W), np.clip(kc - c + NA_WIN_W - 1, 0, nc - 1), -1)
    return pl.pallas_call(
        _abias_kernel,
        grid=(nl * nh,),
        in_specs=[pl.BlockSpec(memory_space=pltpu.SMEM), _const_spec(dcm.shape)],
        out_specs=pl.BlockSpec((1, 5, A_WIN, A_PAIR), lambda i: (i, 0, 0, 0)),
        out_shape=jax.ShapeDtypeStruct((nl * nh, 5, A_WIN, A_PAIR), F32),
        compiler_params=_cparams(("arbitrary",)),
        name="rpb_bias",
    )(rpb_a.reshape(nl * nh, nr * nc).astype(F32), jnp.asarray(dcm, jnp.int32))


def _in_weights(w_in_l, qkn_a, qkn_b, qkn_c, qkn_d):
    a0, b0, c0, d0 = 0, 768, 1280, 2432
    segs = ((a0, 256), (b0, 256), (d0, 256),
            (d0 + 512, 256), (a0 + 512, 256), (b0 + 384, 128),
            (d0 + 256, 256), (a0 + 256, 256), (b0 + 256, 128),
            (c0, 768))
    wt = jnp.concatenate([w_in_l[:, s0:s0 + w] for s0, w in segs], axis=1).T.astype(BF16)
    wvc = w_in_l[:, c0 + 768:c0 + 1152].astype(BF16)
    qs64 = HEAD_DIM ** -0.5 * LOG2E
    qs32 = DIFF_HALF ** -0.5 * LOG2E
    ones = jnp.ones((VT_ROWS,), F32)
    gs = jnp.concatenate([
        jnp.tile(qkn_a[0] * qs64, 4), jnp.tile(qkn_b[0] * qs64, 4), jnp.tile(qkn_d[0] * qs32, 8),
        ones,
        jnp.tile(qkn_d[1], 8), jnp.tile(qkn_a[1], 4), jnp.tile(qkn_b[1], 2),
        jnp.tile(qkn_c[0] * qs64, 6), jnp.tile(qkn_c[1], 6),
    ]).astype(F32).reshape(WT_ROWS, 1)
    return wt, wvc, gs


def _ffn_weights(w_up_l, conv_w_l, conv_b_l, w_down_l):
    d = w_up_l.shape[0]
    wup = w_up_l.reshape(d, 2 * N_FF_CHUNKS, FF_CHUNK).transpose(1, 0, 2).astype(BF16)
    cw = jnp.concatenate([conv_w_l, conv_b_l[None], jnp.zeros((4, 2 * D_FF), F32)], axis=0)
    cw = cw.reshape(8, 2 * N_FF_CHUNKS, FF_CHUNK).transpose(1, 0, 2)
    wdn = w_down_l.reshape(N_FF_CHUNKS, FF_CHUNK, d).astype(BF16)
    return wup, cw, wdn


def _layer(xs, mod, l, p, tables, split_out=None):
    tbl_a, tbl_d, tbl_b, tbl_c = tables
    wt, wvc, gs = _in_weights(p["w_in"][l], p["qkn_a"][l], p["qkn_b"][l], p["qkn_c"][l], p["qkn_d"][l])
    qT, vT, k, qkv_c = _in_call(xs, mod, p["norm_attn_g"][l].reshape(1, -1), wt, wvc, gs)
    lambda_init = 0.8 - 0.6 * math.exp(-0.3 * l)
    o_a = _a_call(qT, k, vT, tbl_a, l)
    o_b = _b_call(p["sink_b"][l], qT, k, vT, tbl_b)
    o_c, lse_c = [], []
    for g, (_, r) in enumerate(DIL_PAIRS):
        o, lse = _c_call(qkv_c, tbl_c[g], g, r)
        o_c.append(o)
        lse_c.append(lse)
    o_d = _d_call(_d_score_bound(p["qkn_d"][l], p["t5_table"]), p["lam_d"][l],
                  p["subln_d"][l].reshape(HEAD_DIM, 1), qT, k, vT, tbl_d, lambda_init)
    x = _out_call(xs, mod, o_a, o_b, o_d, o_c, lse_c, p["w_out"][l].astype(BF16))
    wup, cw, wdn = _ffn_weights(p["w_up"][l], p["conv_w"][l], p["conv_b"][l], p["w_down"][l])
    return _ffn_call(x, mod, p["norm_ffn_g"][l].reshape(1, -1), wup, cw, wdn, split_out)


def _trunk(xs, c, p, depth=DEPTH):
    bt, d = c.shape
    mod_all = _ada_call(c, p["w_ada"], p["b_ada"]).reshape(DEPTH, bt, 6, d)
    tables = (_bias_tables_a(p["rpb_a"]),) + _bias_tables_t5(p["t5_table"])
    split = xs[0].shape[0] if len(xs) == 2 else None
    x = list(xs)
    for l in range(depth):
        x = _layer(x, mod_all[l], l, p, tables, split if l == depth - 1 else None)
        x = list(x) if isinstance(x, (list, tuple)) else [x]
    return tuple(x)


def kernel(x_prompt, x_sample, c_prompt, c_sample, norm_attn_g, norm_ffn_g, w_ada, b_ada, w_in, qkn_a, qkn_b, qkn_c, qkn_d, rpb_a, sink_b, t5_table, lam_d, subln_d, w_out, w_up, conv_w, conv_b, w_down):
    p = dict(norm_attn_g=norm_attn_g, norm_ffn_g=norm_ffn_g, w_ada=w_ada, b_ada=b_ada, w_in=w_in,
             qkn_a=qkn_a, qkn_b=qkn_b, qkn_c=qkn_c, qkn_d=qkn_d, rpb_a=rpb_a, sink_b=sink_b,
             t5_table=t5_table, lam_d=lam_d, subln_d=subln_d, w_out=w_out, w_up=w_up,
             conv_w=conv_w, conv_b=conv_b, w_down=w_down)
    c = jnp.concatenate([c_prompt, c_sample], axis=0)
    return _trunk((x_prompt, x_sample), c, p)
```

```python
import functools
import math

import numpy as np
import jax
import jax.numpy as jnp
from jax import lax
from jax.experimental import pallas as pl
from jax.experimental.pallas import tpu as pltpu

D_MODEL = 1024
SEQ = 2048
DEPTH = 4
HEAD_DIM = 64
GRID_W = 64
NA_HEADS = 4
NA_WIN_H = 8
NA_WIN_W = 16
SW_HEADS = 4
SW_KV_HEADS = 2
SW_HALF_WINDOW = 128
DIL_PAIRS = ((128, 1), (512, 4), (2048, 16))
DIFF_HEADS = 4
DIFF_HALF = HEAD_DIM // 2
T5_BUCKETS = 32
T5_MAX_DIST = 128
T5_B0 = 0
T5_C0 = 4
T5_D0 = 10
D_FF = 2816
RMS_EPS = 1e-6
NEG_INF = -1e30
LOG2E = 1.4426950408889634

F32 = jnp.float32
BF16 = jnp.bfloat16

VMEM_LIMIT_BYTES = 56 * 1024 * 1024
TN_IN = 512
TN_OUT = 512
TN_FFN = 512
FF_CHUNK = 256
N_FF_CHUNKS = D_FF // FF_CHUNK
HALO = 8
TQ_D = 512
D_SAFE_BOUND = 60.0
KB_D = 256
D_NEAR = TQ_D // KB_D + 2
TQ_B = 128
TQ_C = 128
A_PAIR = 2 * GRID_W
A_WIN = 10 * GRID_W

QT_ROWS = 768
VT_ROWS = 640
K_ROWS = 640
C_QK_ROWS = 768
WT_ROWS = QT_ROWS + VT_ROWS + K_ROWS + C_QK_ROWS
C_WIDTH = 1152


def _cparams(sem):
    return pltpu.CompilerParams(dimension_semantics=sem, vmem_limit_bytes=VMEM_LIMIT_BYTES)


def _const_spec(shape):
    nd = len(shape)
    return pl.BlockSpec(shape, lambda *_: (0,) * nd, pipeline_mode=pl.Buffered(1))


def _ada_kernel(c_ref, w_ref, b_ref, o_ref):
    c = c_ref[...]
    sc = (c * (1.0 / (1.0 + jnp.exp(-c)))).astype(BF16)
    o_ref[0] = jnp.dot(sc, w_ref[0].astype(BF16), preferred_element_type=F32) + b_ref[0]


def _ada_call(c, w_ada, b_ada):
    bt = c.shape[0]
    ncol = 6 * D_MODEL
    cb = 1024
    return pl.pallas_call(
        _ada_kernel,
        grid=(DEPTH, ncol // cb),
        in_specs=[
            pl.BlockSpec((bt, D_MODEL), lambda l, j: (0, 0)),
            pl.BlockSpec((1, D_MODEL, cb), lambda l, j: (l, 0, j)),
            pl.BlockSpec((1, 1, cb), lambda l, j: (l, 0, j)),
        ],
        out_specs=pl.BlockSpec((1, bt, cb), lambda l, j: (l, 0, j)),
        out_shape=jax.ShapeDtypeStruct((DEPTH, bt, ncol), F32),
        compiler_params=_cparams(("arbitrary", "arbitrary")),
        name="ada_mod",
    )(c, w_ada, b_ada.reshape(DEPTH, 1, ncol))


def _mod_norm(x, g, scale, shift):
    ms = jnp.mean(x * x, axis=-1, keepdims=True)
    return x * lax.rsqrt(ms + RMS_EPS) * (g * (1.0 + scale)) + shift


def _pick_group(refs, nb_first):
    if len(refs) == 1:
        return refs[0][0]
    return jnp.where(pl.program_id(0) < nb_first, refs[0][0], refs[1][0])


def _group_specs(xs, tn):
    d = xs[0].shape[-1]
    if len(xs) == 1:
        return [pl.BlockSpec((1, tn, d), lambda b, t: (b, t, 0))]
    nb = xs[0].shape[0]
    last = xs[0].shape[1] // tn - 1
    return [pl.BlockSpec((1, tn, d), lambda b, t: (jnp.minimum(b, nb - 1), jnp.where(b < nb, t, last), 0)),
            pl.BlockSpec((1, tn, d), lambda b, t: (jnp.maximum(b - nb, 0), jnp.where(b < nb, 0, t), 0))]


def _in_kernel(*refs, nsrc, nb_first):
    x_refs, (mod_ref, g_ref, wt_ref, wvc_ref, gs_ref, qT_ref, vT_ref, k_ref, c_ref, h_scr) = \
        refs[:nsrc], refs[nsrc:]
    tn = x_refs[0].shape[1]
    nsub = 2
    ts = tn // nsub
    x = _pick_group(x_refs, nb_first)

    def norm_sub(i):
        h = _mod_norm(x[i * ts:(i + 1) * ts], g_ref[...], mod_ref[0, 1:2, :], mod_ref[0, 0:1, :])
        h_scr[i * ts:(i + 1) * ts, :] = h.astype(BF16)

    def head_norm(y, r0, grp):
        rows = y.shape[0]
        y3 = y.reshape(rows // grp, grp, ts)
        ms = jnp.sum(y3 * y3, axis=1, keepdims=True) * (1.0 / grp)
        gs = gs_ref[r0:r0 + rows, :].reshape(rows // grp, grp, 1)
        return (y3 * lax.rsqrt(ms + RMS_EPS) * gs).reshape(rows, ts)

    items = [(r0, 256, grp, ("qT", r0)) for r0, grp in ((0, HEAD_DIM), (256, HEAD_DIM), (512, DIFF_HALF))]
    items += [(QT_ROWS + r0, rows, None, ("vT", r0)) for r0, rows in ((0, 256), (256, 256), (512, 128))]
    base = QT_ROWS + VT_ROWS
    items += [(base + r0, rows, grp, ("k", r0))
              for r0, rows, grp in ((0, 256, DIFF_HALF), (256, 256, HEAD_DIM), (512, 128, HEAD_DIM))]
    base += K_ROWS
    items += [(base + r0, 256, HEAD_DIM, ("c", r0)) for r0 in (0, 256, 512)]
    items += [(None, C_WIDTH - C_QK_ROWS, None, ("c", C_QK_ROWS))]

    def matmul(item, i):
        w0, rows = item[0], item[1]
        hb = h_scr[i * ts:(i + 1) * ts, :]
        if w0 is None:
            return jnp.dot(hb, wvc_ref[...], preferred_element_type=F32)
        return lax.dot_general(wt_ref[w0:w0 + rows, :], hb, (((1,), (1,)), ((), ())),
                               preferred_element_type=F32)

    def finish(item, i, y):
        w0, rows, grp, (dst, r0) = item
        tok = slice(i * ts, (i + 1) * ts)
        if grp is not None:
            y = head_norm(y, w0, grp)
        if dst == "qT":
            qT_ref[0, r0:r0 + rows, tok] = y.astype(BF16)
        elif dst == "vT":
            vT_ref[0, r0:r0 + rows, tok] = y.astype(BF16)
        elif dst == "k":
            k_ref[0, tok, r0:r0 + rows] = y.T.astype(BF16)
        else:
            c_ref[0, tok, r0:r0 + rows] = y if w0 is None else y.T

    work = [(item, i) for i in range(nsub) for item in items]
    norm_sub(0)
    pending = None
    for n_issued, (item, i) in enumerate(work):
        if n_issued == 2:
            for j in range(1, nsub):
                norm_sub(j)
        y = matmul(item, i)
        if pending is not None:
            finish(*pending)
        pending = (item, i, y)
    finish(*pending)


def _in_call(xs, mod, g, wt, wvc, gs):
    bt = sum(x.shape[0] for x in xs)
    _, n, d = xs[0].shape
    tn = TN_IN
    return pl.pallas_call(
        functools.partial(_in_kernel, nsrc=len(xs), nb_first=xs[0].shape[0]),
        grid=(bt, n // tn),
        in_specs=_group_specs(xs, tn) + [
            pl.BlockSpec((1, 6, d), lambda b, t: (b, 0, 0)),
            _const_spec((1, d)),
            _const_spec((WT_ROWS, d)),
            _const_spec((d, C_WIDTH - C_QK_ROWS)),
            _const_spec((WT_ROWS, 1)),
        ],
        out_specs=[
            pl.BlockSpec((1, QT_ROWS, tn), lambda b, t: (b, 0, t)),
            pl.BlockSpec((1, VT_ROWS, tn), lambda b, t: (b, 0, t)),
            pl.BlockSpec((1, tn, K_ROWS), lambda b, t: (b, t, 0)),
            pl.BlockSpec((1, tn, C_WIDTH), lambda b, t: (b, t, 0)),
        ],
        out_shape=[
            jax.ShapeDtypeStruct((bt, QT_ROWS, n), BF16),
            jax.ShapeDtypeStruct((bt, VT_ROWS, n), BF16),
            jax.ShapeDtypeStruct((bt, n, K_ROWS), BF16),
            jax.ShapeDtypeStruct((bt, n, C_WIDTH), F32),
        ],
        scratch_shapes=[pltpu.VMEM((tn, d), BF16)],
        compiler_params=_cparams(("arbitrary", "arbitrary")),
        name="in_proj",
    )(*xs, mod, g, wt, wvc, gs)


def _d_kernel(bound_ref, lam_ref, sub_ref, q_ref, k_ref, v_ref, tbl_ref, o_ref, s_scr, p_scr, *, lambda_init):
    t = pl.program_id(1)
    tq = q_ref.shape[2]
    nkb = k_ref.shape[1] // KB_D
    qT = q_ref[0]
    rowgrp = lax.broadcasted_iota(jnp.int32, qT.shape, 0) // DIFF_HALF
    lp = lam_ref[...]
    lam = (jnp.exp(jnp.sum(lp[0:1] * lp[1:2], axis=-1, keepdims=True))
           - jnp.exp(jnp.sum(lp[2:3] * lp[3:4], axis=-1, keepdims=True)) + lambda_init)

    first = t * (tq // KB_D)
    blocks = []
    for d in range(-1, nkb - 1):
        jb = lax.rem(first + d + nkb, nkb)
        blocks.append((d < D_NEAR - 1, pl.multiple_of(jb * KB_D, KB_D),
                       jnp.clip(jb - first, -2, D_NEAR - 1) + 2))

    nblk = len(blocks)
    vmax = lambda s: jnp.max(jnp.max(s.reshape(KB_D // 8, 8, tq), axis=0), axis=0, keepdims=True)

    def scores(h, pos, st):
        near, k0, idx = blocks[pos]
        kb = k_ref[0, pl.ds(k0, KB_D), :]
        for i in range(2):
            slot = 2 * (h % 2) + i
            s = jnp.dot(kb, st["q"][i], preferred_element_type=F32)
            if near:
                s = s + tbl_ref[h, idx]
                s_scr[slot, pos] = s
                st["m"][i] = jnp.maximum(st["m"][i], vmax(s))
            else:
                s_scr[slot, pos] = s
                st["m"][i] = jnp.maximum(st["m"][i], vmax(s) + tbl_ref[h, idx, 0:1, :])

    def probs(h, pos, st):
        near, k0, idx = blocks[pos]
        for i in range(2):
            slot = 2 * (h % 2) + i
            shift = st["m"][i] if near else st["m"][i] - tbl_ref[h, idx, 0:1, :]
            p_scr[slot, pl.ds(k0, KB_D), :] = jnp.exp2(s_scr[slot, pos] - shift).astype(BF16)

    def new_state(h):
        return {"q": [jnp.where(rowgrp == 2 * h + i, qT, jnp.zeros_like(qT)) for i in range(2)],
                "m": [jnp.full((1, tq), -jnp.inf, F32)] * 2}

    ones = jnp.ones((16, k_ref.shape[1]), BF16)

    def finish(h, st):
        vh = jnp.concatenate([v_ref[0, h * HEAD_DIM:(h + 1) * HEAD_DIM, :], ones], axis=0)
        o = []
        for i in range(2):
            ol = jnp.dot(vh, p_scr[2 * (h % 2) + i], preferred_element_type=F32)
            o.append(ol[:HEAD_DIM] * (1.0 / ol[HEAD_DIM:HEAD_DIM + 1]))
        oh = o[0] - lam * o[1]
        ms = jnp.mean(oh * oh, axis=0, keepdims=True)
        return oh * lax.rsqrt(ms + RMS_EPS) * sub_ref[...] * (1.0 - lambda_init)

    def exact_max_path():
        st = new_state(0)
        for pos in range(nblk):
            scores(0, pos, st)
        parts = []
        for h in range(DIFF_HEADS):
            nxt = new_state(h + 1) if h + 1 < DIFF_HEADS else None
            for pos in range(nblk):
                if nxt is not None:
                    scores(h + 1, pos, nxt)
                probs(h, pos, st)
            parts.append(finish(h, st))
            st = nxt
        o_ref[0] = jnp.concatenate(parts, axis=0).T.astype(BF16)

    def bounded_path():
        bound = bound_ref[0]

        def block(h, st, pos):
            near, k0, idx = blocks[pos]
            kb = k_ref[0, pl.ds(k0, KB_D), :]
            for i in range(2):
                s = jnp.dot(kb, st["q"][i], preferred_element_type=F32)
                if near:
                    p = jnp.exp2(s + tbl_ref[h, idx] - bound)
                else:
                    p = jnp.exp2(s - (bound - tbl_ref[h, idx, 0:1, :]))
                p_scr[2 * (h % 2) + i, pl.ds(k0, KB_D), :] = p.astype(BF16)

        lag = 3
        parts = []
        states = [new_state(h) for h in range(DIFF_HEADS)]
        for h in range(DIFF_HEADS):
            for pos in range(nblk):
                block(h, states[h], pos)
                if h > 0 and pos == lag - 1:
                    parts.append(finish(h - 1, states[h - 1]))
        parts.append(finish(DIFF_HEADS - 1, states[-1]))
        o_ref[0] = jnp.concatenate(parts, axis=0).T.astype(BF16)

    pl.when(bound_ref[0] <= D_SAFE_BOUND)(bounded_path)
    pl.when(bound_ref[0] > D_SAFE_BOUND)(exact_max_path)


def _d_score_bound(qkn_d_l, t5_table):
    gq = jnp.max(jnp.abs(qkn_d_l[0])) * (DIFF_HALF ** -0.5 * LOG2E)
    gk = jnp.max(jnp.abs(qkn_d_l[1]))
    bias = jnp.max(jnp.abs(t5_table[:, T5_D0:T5_D0 + DIFF_HEADS])) * LOG2E
    return (1.02 * DIFF_HALF * gq * gk + bias).astype(F32).reshape(1)


def _d_call(bound, lam_p, subln, qT, k, vT, tbl, lambda_init):
    bt, _, n = qT.shape
    tq = TQ_D
    return pl.pallas_call(
        functools.partial(_d_kernel, lambda_init=lambda_init),
        grid=(bt, n // tq),
        in_specs=[
            pl.BlockSpec(memory_space=pltpu.SMEM),
            _const_spec((4, DIFF_HALF)),
            _const_spec((HEAD_DIM, 1)),
            pl.BlockSpec((1, 256, tq), lambda b, t: (b, 2, t)),
            pl.BlockSpec((1, n, 256), lambda b, t: (b, 0, 0)),
            pl.BlockSpec((1, 256, n), lambda b, t: (b, 0, 0)),
            _const_spec((DIFF_HEADS, D_NEAR + 2, KB_D, tq)),
        ],
        out_specs=pl.BlockSpec((1, tq, 256), lambda b, t: (b, t, 0)),
        out_shape=jax.ShapeDtypeStruct((bt, n, 256), BF16),
        scratch_shapes=[
            pltpu.VMEM((4, n // KB_D, KB_D, tq), F32),
            pltpu.VMEM((4, n, tq), BF16),
        ],
        compiler_params=_cparams(("parallel", "arbitrary")),
        name="mixer_d",
    )(bound, lam_p, subln, qT, k, vT, tbl)


def _b_kernel(sink_ref, q_ref, k_ref, v_ref, bias_ref, o_ref):
    n = k_ref.shape[1]
    tq = TQ_B
    nt = n // tq
    grp = SW_HEADS // SW_KV_HEADS
    zero = jnp.zeros((HEAD_DIM, tq), BF16)
    sinks = [sink_ref[h] * LOG2E for h in range(SW_HEADS)]
    group = 4
    for t0 in range(0, nt, group):
        chains = []
        for t in range(t0, t0 + group):
            qT = q_ref[0, :, t * tq:(t + 1) * tq]
            qcat = []
            for h in range(SW_HEADS):
                qh = qT[h * HEAD_DIM:(h + 1) * HEAD_DIM]
                qcat.append(jnp.concatenate([qh, zero] if h // grp == 0 else [zero, qh], axis=0))
            qcat = jnp.concatenate(qcat, axis=1)
            blocks = [j for j in (-1, 0, 1) if 0 <= t + j < nt]
            s_all = [jnp.dot(k_ref[0, (t + j) * tq:(t + j + 1) * tq, :], qcat,
                             preferred_element_type=F32) for j in blocks]
            for h in range(SW_HEADS):
                chains.append((t, h, [(s[:, h * tq:(h + 1) * tq] + bias_ref[h, j + 1], t + j)
                                      for s, j in zip(s_all, blocks)]))
        m_l = []
        for t, h, ss in chains:
            m = jnp.full((1, tq), sinks[h], F32)
            for s, _ in ss:
                m = jnp.maximum(m, jnp.max(s, axis=0, keepdims=True))
            m_l.append(m)
        p_l = [[jnp.exp2(s - m) for s, _ in ss] for (t, h, ss), m in zip(chains, m_l)]
        outs = []
        for (t, h, ss), m, ps in zip(chains, m_l, p_l):
            kv = h // grp
            l = jnp.exp2(sinks[h] - m)
            o = jnp.zeros((HEAD_DIM, tq), F32)
            for p, (_, kb) in zip(ps, ss):
                l = l + jnp.sum(p, axis=0, keepdims=True)
                vh = v_ref[0, kv * HEAD_DIM:(kv + 1) * HEAD_DIM, kb * tq:(kb + 1) * tq]
                o = o + jnp.dot(vh, p.astype(BF16), preferred_element_type=F32)
            outs.append(o * (1.0 / l))
        for i, t in enumerate(range(t0, t0 + group)):
            o_ref[0, t * tq:(t + 1) * tq, :] = jnp.concatenate(
                outs[i * SW_HEADS:(i + 1) * SW_HEADS], axis=0).T.astype(BF16)


def _b_call(sink, qT, k, vT, bias):
    bt, _, n = qT.shape
    return pl.pallas_call(
        _b_kernel,
        grid=(bt,),
        in_specs=[
            pl.BlockSpec(memory_space=pltpu.SMEM),
            pl.BlockSpec((1, 256, n), lambda b: (b, 1, 0)),
            pl.BlockSpec((1, n, 128), lambda b: (b, 0, 4)),
            pl.BlockSpec((1, 128, n), lambda b: (b, 4, 0)),
            _const_spec((SW_HEADS, 3, TQ_B, TQ_B)),
        ],
        out_specs=pl.BlockSpec((1, n, 256), lambda b: (b, 0, 0)),
        out_shape=jax.ShapeDtypeStruct((bt, n, 256), BF16),
        compiler_params=_cparams(("parallel",)),
        name="mixer_b",
    )(sink, qT, k, vT, bias)


def _a_kernel(q_ref, k_ref, v_ref, bias_ref, o_ref):
    n = k_ref.shape[1]
    rows = n // GRID_W
    npair = n // A_PAIR
    kh = min(NA_WIN_H, rows)
    rowhead = lax.broadcasted_iota(jnp.int32, (NA_HEADS * HEAD_DIM, A_PAIR), 0) // HEAD_DIM
    group = 8
    for g0 in range(0, npair, group):
        geo, s_all = [], []
        for rp in range(g0, g0 + group):
            r = 2 * rp
            ws = min(max(r - kh // 2, 0), rows - kh, rows - A_WIN // GRID_W)
            var = rp if rp < 2 else (2 if rp < npair - 2 else rp - (npair - 5))
            q0, k0 = rp * A_PAIR, ws * GRID_W
            qT = q_ref[0, :, q0:q0 + A_PAIR]
            qcat = jnp.concatenate([jnp.where(rowhead == h, qT, jnp.zeros_like(qT))
                                    for h in range(NA_HEADS)], axis=1)
            s_all.append(jnp.dot(k_ref[0, k0:k0 + A_WIN, :], qcat, preferred_element_type=F32))
            geo.append((q0, k0, var))
        chains = [(i, h) for i in range(group) for h in range(NA_HEADS)]
        s_l = [s_all[i][:, h * A_PAIR:(h + 1) * A_PAIR] + bias_ref[h, geo[i][2]] for i, h in chains]
        m_l = [jnp.max(s, axis=0, keepdims=True) for s in s_l]
        p_l = [jnp.exp2(s - m) for s, m in zip(s_l, m_l)]
        l_l = [jnp.sum(p, axis=0, keepdims=True) for p in p_l]
        o_l = [jnp.dot(v_ref[0, h * HEAD_DIM:(h + 1) * HEAD_DIM, geo[i][1]:geo[i][1] + A_WIN],
                       p.astype(BF16), preferred_element_type=F32) * (1.0 / l)
               for (i, h), p, l in zip(chains, p_l, l_l)]
        for i in range(group):
            q0 = geo[i][0]
            o_ref[0, q0:q0 + A_PAIR, :] = jnp.concatenate(
                o_l[i * NA_HEADS:(i + 1) * NA_HEADS], axis=0).T.astype(BF16)


def _a_call(qT, k, vT, bias, layer):
    bt, _, n = qT.shape
    return pl.pallas_call(
        _a_kernel,
        grid=(bt,),
        in_specs=[
            pl.BlockSpec((1, 256, n), lambda b: (b, 0, 0)),
            pl.BlockSpec((1, n, 256), lambda b: (b, 0, 1)),
            pl.BlockSpec((1, 256, n), lambda b: (b, 1, 0)),
            pl.BlockSpec((NA_HEADS, 5, A_WIN, A_PAIR), lambda b: (layer, 0, 0, 0),
                         pipeline_mode=pl.Buffered(1)),
        ],
        out_specs=pl.BlockSpec((1, n, 256), lambda b: (b, 0, 0)),
        out_shape=jax.ShapeDtypeStruct((bt, n, 256), BF16),
        compiler_params=_cparams(("parallel",)),
        name="mixer_a",
    )(qT, k, vT, bias)


def _c_kernel(q_ref, k_ref, v_ref, bias_ref, o_ref, lse_ref, sub_scr, o_scr, lse_scr, *, dil):
    n = q_ref.shape[1]
    sub_len = n // dil
    win = min(2 * TQ_C, sub_len)
    nqb = sub_len // TQ_C
    lane_head = lax.broadcasted_iota(jnp.int32, (TQ_C, 2 * HEAD_DIM), 1) // HEAD_DIM

    def gather(p, carry):
        src = pl.ds(p, sub_len, stride=dil)
        dst = pl.ds(pl.multiple_of(p * sub_len, sub_len), sub_len)
        for i, ref in enumerate((q_ref, k_ref, v_ref)):
            sub_scr[i, dst, :] = ref[0, src, :].astype(BF16)
        return carry

    if dil > 1:
        lax.fori_loop(0, dil, gather, 0)
    else:
        for i, ref in enumerate((q_ref, k_ref, v_ref)):
            sub_scr[i] = ref[0].astype(BF16)

    group = 4
    for g0 in range(0, n // TQ_C, group):
        geo = []
        for item in range(g0, g0 + group):
            p, qb = divmod(item, nqb)
            q0 = p * sub_len + qb * TQ_C
            start = p * sub_len + min(max(qb * TQ_C - TQ_C // 2, 0), sub_len - win)
            var = 0 if qb == 0 else (2 if qb == nqb - 1 else 1)
            geo.extend((q0, start, var, j) for j in range(2))
        s_l = []
        for q0, start, var, j in geo:
            qblk = sub_scr[0, q0:q0 + TQ_C, :]
            qm = jnp.where(lane_head == j, qblk, jnp.zeros_like(qblk))
            s = lax.dot_general(qm, sub_scr[1, start:start + win, :], (((1,), (1,)), ((), ())),
                                preferred_element_type=F32)
            s_l.append(s + bias_ref[j, var])
        m_l = [jnp.max(s, axis=-1, keepdims=True) for s in s_l]
        p_l = [jnp.exp2(s - m) for s, m in zip(s_l, m_l)]
        l_l = [jnp.sum(pr, axis=-1, keepdims=True) for pr in p_l]
        o_l = [jnp.dot(pr.astype(BF16), sub_scr[2, start:start + win, :], preferred_element_type=F32)
               for pr, (q0, start, var, j) in zip(p_l, geo)]
        for i in range(0, len(geo), 2):
            q0 = geo[i][0]
            o0, o1 = o_l[i] * (1.0 / l_l[i]), o_l[i + 1] * (1.0 / l_l[i + 1])
            e0 = jnp.broadcast_to(m_l[i] + jnp.log2(l_l[i]), (TQ_C, 2 * HEAD_DIM))
            e1 = jnp.broadcast_to(m_l[i + 1] + jnp.log2(l_l[i + 1]), (TQ_C, 2 * HEAD_DIM))
            o_scr[q0:q0 + TQ_C, :] = jnp.where(lane_head == 0, o0, o1)
            lse_scr[q0:q0 + TQ_C, :] = jnp.where(lane_head == 0, e0, e1)

    def scatter(p, carry):
        dst = pl.ds(p, sub_len, stride=dil)
        src = pl.ds(pl.multiple_of(p * sub_len, sub_len), sub_len)
        o_ref[0, dst, :] = o_scr[src, :]
        lse_ref[0, dst, :] = lse_scr[src, :]
        return carry

    if dil > 1:
        lax.fori_loop(0, dil, scatter, 0)
    else:
        o_ref[0] = o_scr[...]
        lse_ref[0] = lse_scr[...]


def _c_call(qkv, bias, g, dil):
    bt, n, _ = qkv.shape
    sub_len = n // dil
    win = min(2 * TQ_C, sub_len)
    lanes = 2 * HEAD_DIM
    return pl.pallas_call(
        functools.partial(_c_kernel, dil=dil),
        grid=(bt,),
        in_specs=[
            pl.BlockSpec((1, n, lanes), lambda b: (b, 0, g)),
            pl.BlockSpec((1, n, lanes), lambda b: (b, 0, 3 + g)),
            pl.BlockSpec((1, n, lanes), lambda b: (b, 0, 6 + g)),
            _const_spec((2, 3, TQ_C, win)),
        ],
        out_specs=[
            pl.BlockSpec((1, n, lanes), lambda b: (b, 0, 0)),
            pl.BlockSpec((1, n, lanes), lambda b: (b, 0, 0)),
        ],
        out_shape=[
            jax.ShapeDtypeStruct((bt, n, lanes), F32),
            jax.ShapeDtypeStruct((bt, n, lanes), F32),
        ],
        scratch_shapes=[
            pltpu.VMEM((3, n, lanes), BF16),
            pltpu.VMEM((n, lanes), F32),
            pltpu.VMEM((n, lanes), F32),
        ],
        compiler_params=_cparams(("parallel",)),
        name=f"mixer_c{g}",
    )(qkv, qkv, qkv, bias)


def _out_kernel(*refs, nsrc, nb_first):
    x_refs, (mod_ref, oa_ref, ob_ref, od_ref, oc0_ref, oc1_ref, oc2_ref,
             l0_ref, l1_ref, l2_ref, w_ref, o_ref) = refs[:nsrc], refs[nsrc:]
    l0, l1, l2 = l0_ref[0], l1_ref[0], l2_ref[0]
    mx = jnp.maximum(jnp.maximum(l0, l1), l2)
    e0, e1, e2 = jnp.exp2(l0 - mx), jnp.exp2(l1 - mx), jnp.exp2(l2 - mx)
    inv = 1.0 / (e0 + e1 + e2)
    mixed = jnp.concatenate([oa_ref[0], ob_ref[0],
                             (oc0_ref[0] * (e0 * inv)).astype(BF16),
                             (oc1_ref[0] * (e1 * inv)).astype(BF16),
                             (oc2_ref[0] * (e2 * inv)).astype(BF16), od_ref[0]], axis=1)
    y = jnp.dot(mixed, w_ref[...], preferred_element_type=F32)
    o_ref[0] = _pick_group(x_refs, nb_first) + mod_ref[0, 2:3, :] * y


def _out_call(xs, mod, o_a, o_b, o_d, o_c, lse_c, w_out):
    bt = sum(x.shape[0] for x in xs)
    _, n, d = xs[0].shape
    tn = TN_OUT
    tok = lambda w: pl.BlockSpec((1, tn, w), lambda b, t: (b, t, 0))
    return pl.pallas_call(
        functools.partial(_out_kernel, nsrc=len(xs), nb_first=xs[0].shape[0]),
        grid=(bt, n // tn),
        in_specs=_group_specs(xs, tn) + [pl.BlockSpec((1, 6, d), lambda b, t: (b, 0, 0)),
                  tok(256), tok(256), tok(256), tok(128), tok(128), tok(128),
                  tok(128), tok(128), tok(128), _const_spec(w_out.shape)],
        out_specs=tok(d),
        out_shape=jax.ShapeDtypeStruct((bt, n, d), F32),
        compiler_params=_cparams(("arbitrary", "arbitrary")),
        name="out_proj",
    )(*xs, mod, o_a, o_b, o_d, *o_c, *lse_c, w_out)


def _ffn_kernel(xp_ref, x_ref, xn_ref, mod_ref, g_ref, wup_ref, cw_ref, wdn_ref, *refs, nb_first):
    o_refs, (h_scr, acc_scr, act_scr) = refs[:-3], refs[-3:]
    t = pl.program_id(1)
    nt = pl.num_programs(1)
    tn = x_ref.shape[1]
    ext = tn + 2 * HALO
    xe = jnp.concatenate([xp_ref[0], x_ref[0], xn_ref[0]], axis=0)
    h = _mod_norm(xe, g_ref[...], mod_ref[0, 4:5, :], mod_ref[0, 3:4, :])
    row = lax.broadcasted_iota(jnp.int32, (ext, 1), 0)
    keep = jnp.logical_and(jnp.logical_or(row >= HALO, t > 0),
                           jnp.logical_or(row < HALO + tn, t < nt - 1))
    h_scr[...] = jnp.where(keep, h, 0.0).astype(BF16)

    nsplit = 4
    rows = tn // nsplit

    def conv(u, cw, r0):
        ue = u[r0:r0 + rows + 2 * HALO]
        prev = pltpu.roll(ue, 1, 0)[HALO:HALO + rows]
        nxt = pltpu.roll(ue, rows + 2 * HALO - 1, 0)[HALO:HALO + rows]
        return prev * cw[0:1] + ue[HALO:HALO + rows] * cw[1:2] + nxt * cw[2:3] + cw[3:4]

    hb = h_scr[...]
    up = lambda c: jnp.dot(hb, wup_ref[c], preferred_element_type=F32)
    u_val, u_gate = up(0), up(N_FF_CHUNKS)
    kgroup = 6
    for c in range(N_FF_CHUNKS):
        nxt_u = []
        g0 = c - c % kgroup
        gl = min(kgroup, N_FF_CHUNKS - g0)
        last_of_group = c == g0 + gl - 1
        col = (c - g0) * FF_CHUNK
        for part in range(nsplit):
            r0 = part * rows
            if c + 1 < N_FF_CHUNKS and part % (nsplit // 2) == 0:
                nxt_u.append(up(c + 1 + len(nxt_u) * N_FF_CHUNKS))
            val = conv(u_val, cw_ref[c], r0)
            gate = conv(u_gate, cw_ref[N_FF_CHUNKS + c], r0)
            act_scr[r0:r0 + rows, col:col + FF_CHUNK] = (val * gate * jax.nn.sigmoid(gate)).astype(BF16)
            if last_of_group:
                wd = wdn_ref[g0:g0 + gl].reshape(gl * FF_CHUNK, wdn_ref.shape[2])
                y = jnp.dot(act_scr[r0:r0 + rows, 0:gl * FF_CHUNK], wd, preferred_element_type=F32)
                if g0 == 0:
                    acc_scr[r0:r0 + rows, :] = y
                else:
                    acc_scr[r0:r0 + rows, :] += y
        if nxt_u:
            u_val, u_gate = nxt_u
    if len(o_refs) == 1:
        o_refs[0][0] = x_ref[0] + mod_ref[0, 5:6, :] * acc_scr[...]
    else:
        first = pl.program_id(0) < nb_first

        @pl.when(first)
        def _():
            o_refs[0][0] = x_ref[0] + mod_ref[0, 5:6, :] * acc_scr[...]

        @pl.when(jnp.logical_not(first))
        def _():
            o_refs[1][0] = x_ref[0] + mod_ref[0, 5:6, :] * acc_scr[...]


def _ffn_call(x, mod, g, wup, cw, wdn, nb_first=None):
    bt, n, d = x.shape
    tn = TN_FFN
    nb = tn // HALO
    nhalo = n // HALO
    if nb_first is None:
        out_specs = pl.BlockSpec((1, tn, d), lambda b, t: (b, t, 0))
        out_shape = jax.ShapeDtypeStruct((bt, n, d), F32)
    else:
        last = n // tn - 1
        out_specs = [
            pl.BlockSpec((1, tn, d), lambda b, t: (jnp.minimum(b, nb_first - 1),
                                                   jnp.where(b < nb_first, t, last), 0)),
            pl.BlockSpec((1, tn, d), lambda b, t: (jnp.maximum(b - nb_first, 0),
                                                   jnp.where(b < nb_first, 0, t), 0))]
        out_shape = [jax.ShapeDtypeStruct((nb_first, n, d), F32),
                     jax.ShapeDtypeStruct((bt - nb_first, n, d), F32)]
    return pl.pallas_call(
        functools.partial(_ffn_kernel, nb_first=nb_first),
        grid=(bt, n // tn),
        in_specs=[
            pl.BlockSpec((1, HALO, d), lambda b, t: (b, jnp.maximum(t * nb - 1, 0), 0)),
            pl.BlockSpec((1, tn, d), lambda b, t: (b, t, 0)),
            pl.BlockSpec((1, HALO, d), lambda b, t: (b, jnp.minimum((t + 1) * nb, nhalo - 1), 0)),
            pl.BlockSpec((1, 6, d), lambda b, t: (b, 0, 0)),
            _const_spec((1, d)),
            _const_spec(wup.shape),
            _const_spec(cw.shape),
            _const_spec(wdn.shape),
        ],
        out_specs=out_specs,
        out_shape=out_shape,
        scratch_shapes=[pltpu.VMEM((tn + 2 * HALO, d), BF16), pltpu.VMEM((tn, d), F32),
                        pltpu.VMEM((tn, 6 * FF_CHUNK), BF16)],
        compiler_params=_cparams(("arbitrary", "arbitrary")),
        name="ffn",
    )(x, x, x, mod, g, wup, cw, wdn)


def _t5_bucket(rel):
    nb = T5_BUCKETS // 2
    max_exact = nb // 2
    base = jnp.where(rel > 0, nb, 0)
    n = jnp.abs(rel)
    nf = jnp.maximum(n, 1).astype(jnp.float32)
    large = max_exact + (jnp.log(nf / max_exact) / math.log(T5_MAX_DIST / max_exact)
                         * (nb - max_exact)).astype(jnp.int32)
    large = jnp.minimum(large, nb - 1)
    return base + jnp.where(n < max_exact, n, large)


def _t5_kernel(tab_ref, bkt_ref, o_ref, *, head0):
    h = head0 + pl.program_id(0)
    for v in range(bkt_ref.shape[0]):
        bkt = bkt_ref[v]
        acc = jnp.full(bkt.shape, NEG_INF, F32)
        for b in range(T5_BUCKETS):
            acc = jnp.where(bkt == b, tab_ref[b, h] * LOG2E, acc)
        o_ref[0, v] = acc


def _t5_bias(table, rel, mask, head0, nheads):
    bkt = jnp.where(jnp.asarray(mask), _t5_bucket(jnp.asarray(rel, dtype=jnp.int32)), -1)
    return pl.pallas_call(
        functools.partial(_t5_kernel, head0=head0),
        grid=(nheads,),
        in_specs=[pl.BlockSpec(memory_space=pltpu.SMEM), _const_spec(bkt.shape)],
        out_specs=pl.BlockSpec((1,) + bkt.shape, lambda h: (h, 0, 0, 0)),
        out_shape=jax.ShapeDtypeStruct((nheads,) + bkt.shape, F32),
        compiler_params=_cparams(("arbitrary",)),
        name="t5_bias",
    )(table.astype(F32), bkt)


def _bias_tables_t5(t5_table):
    kk = np.arange(KB_D)[:, None]
    qq = np.arange(TQ_D)[None, :]
    far = 8 * T5_MAX_DIST
    rel_d = np.stack([np.full((KB_D, TQ_D), -far)]
                     + [j * KB_D + kk - qq for j in range(-1, D_NEAR - 1)]
                     + [np.full((KB_D, TQ_D), far)])
    tbl_d = _t5_bias(t5_table, rel_d, np.ones_like(rel_d, bool), T5_D0, DIFF_HEADS)
    kk = np.arange(TQ_B)[:, None]
    qq = np.arange(TQ_B)[None, :]
    rel_b = np.stack([j * TQ_B + kk - qq for j in (-1, 0, 1)])
    tbl_b = _t5_bias(t5_table, rel_b, np.abs(rel_b) <= SW_HALF_WINDOW, T5_B0, SW_HEADS)
    tbl_c = []
    for g, (w, r) in enumerate(DIL_PAIRS):
        sub_len = SEQ // r
        hw = w // (2 * r)
        win = min(2 * TQ_C, sub_len)
        qq = np.arange(TQ_C)[:, None]
        kk = np.arange(win)[None, :]
        offs = (0, -(TQ_C // 2), -(win - TQ_C))
        rel_c = np.stack([o + kk - qq for o in offs])
        tbl_c.append(_t5_bias(t5_table, rel_c * r, np.abs(rel_c) <= hw, T5_C0 + 2 * g, 2))
    return tbl_d, tbl_b, tbl_c


def _a_variants():
    rows = SEQ // GRID_W
    kh = min(NA_WIN_H, rows)
    wrows = A_WIN // GRID_W
    out = []
    for r in (0, 2, 4, rows - 4, rows - 2):
        ws = min(int(np.clip(r - kh // 2, 0, rows - kh)), rows - wrows)
        per_row = []
        for kr in range(wrows):
            pair = []
            for a in range(2):
                rsq = int(np.clip(r + a - kh // 2, 0, rows - kh))
                krow = ws + kr
                pair.append(krow - (r + a) + NA_WIN_H - 1 if rsq <= krow < rsq + kh else None)
            per_row.append(tuple(pair))
        out.append(tuple(per_row))
    return tuple(out)


def _abias_kernel(rpb_ref, dcm_ref, o_ref):
    i = pl.program_id(0)
    ncol = 2 * NA_WIN_W - 1
    dcm = dcm_ref[...]
    second = lax.broadcasted_iota(jnp.int32, dcm.shape, 1) >= GRID_W
    neg = jnp.full(dcm.shape, NEG_INF, F32)
    where = {}
    for v, per_row in enumerate(_a_variants()):
        for kr, pair in enumerate(per_row):
            where.setdefault(pair, []).append((v, kr))
    for (dr0, dr1), places in where.items():
        if dr0 is None and dr1 is None:
            tile = neg
        else:
            def body(t, acc, dr0=dr0, dr1=dr1):
                s0 = NEG_INF if dr0 is None else rpb_ref[i, dr0 * ncol + t] * LOG2E
                s1 = NEG_INF if dr1 is None else rpb_ref[i, dr1 * ncol + t] * LOG2E
                return jnp.where(dcm == t, jnp.where(second, s1, s0), acc)

            tile = lax.fori_loop(0, ncol, body, neg)
        for v, kr in places:
            o_ref[0, v, kr * GRID_W:(kr + 1) * GRID_W, :] = tile


def _bias_tables_a(rpb_a):
    nl, nh, nr, nc = rpb_a.shape
    kc = np.arange(GRID_W)[:, None]
    c = np.tile(np.arange(GRID_W), 2)[None, :]
    cs = np.clip(c - NA_WIN_W // 2, 0, GRID_W - NA_WIN_W)
    dcm = np.where((kc >= cs) & (kc < cs + NA_WIN_W), np.clip(kc - c + NA_WIN_W - 1, 0, nc - 1), -1)
    return pl.pallas_call(
        _abias_kernel,
        grid=(nl * nh,),
        in_specs=[pl.BlockSpec(memory_space=pltpu.SMEM), _const_spec(dcm.shape)],
        out_specs=pl.BlockSpec((1, 5, A_WIN, A_PAIR), lambda i: (i, 0, 0, 0)),
        out_shape=jax.ShapeDtypeStruct((nl * nh, 5, A_WIN, A_PAIR), F32),
        compiler_params=_cparams(("arbitrary",)),
        name="rpb_bias",
    )(rpb_a.reshape(nl * nh, nr * nc).astype(F32), jnp.asarray(dcm, jnp.int32))


def _in_weights(w_in_l, qkn_a, qkn_b, qkn_c, qkn_d):
    a0, b0, c0, d0 = 0, 768, 1280, 2432
    segs = ((a0, 256), (b0, 256), (d0, 256),
            (d0 + 512, 256), (a0 + 512, 256), (b0 + 384, 128),
            (d0 + 256, 256), (a0 + 256, 256), (b0 + 256, 128),
            (c0, 768))
    wt = jnp.concatenate([w_in_l[:, s0:s0 + w] for s0, w in segs], axis=1).T.astype(BF16)
    wvc = w_in_l[:, c0 + 768:c0 + 1152].astype(BF16)
    qs64 = HEAD_DIM ** -0.5 * LOG2E
    qs32 = DIFF_HALF ** -0.5 * LOG2E
    ones = jnp.ones((VT_ROWS,), F32)
    gs = jnp.concatenate([
        jnp.tile(qkn_a[0] * qs64, 4), jnp.tile(qkn_b[0] * qs64, 4), jnp.tile(qkn_d[0] * qs32, 8),
        ones,
        jnp.tile(qkn_d[1], 8), jnp.tile(qkn_a[1], 4), jnp.tile(qkn_b[1], 2),
        jnp.tile(qkn_c[0] * qs64, 6), jnp.tile(qkn_c[1], 6),
    ]).astype(F32).reshape(WT_ROWS, 1)
    return wt, wvc, gs


def _ffn_weights(w_up_l, conv_w_l, conv_b_l, w_down_l):
    d = w_up_l.shape[0]
    wup = w_up_l.reshape(d, 2 * N_FF_CHUNKS, FF_CHUNK).transpose(1, 0, 2).astype(BF16)
    cw = jnp.concatenate([conv_w_l, conv_b_l[None], jnp.zeros((4, 2 * D_FF), F32)], axis=0)
    cw = cw.reshape(8, 2 * N_FF_CHUNKS, FF_CHUNK).transpose(1, 0, 2)
    wdn = w_down_l.reshape(N_FF_CHUNKS, FF_CHUNK, d).astype(BF16)
    return wup, cw, wdn


def _layer(xs, mod, l, p, tables, split_out=None):
    tbl_a, tbl_d, tbl_b, tbl_c = tables
    wt, wvc, gs = _in_weights(p["w_in"][l], p["qkn_a"][l], p["qkn_b"][l], p["qkn_c"][l], p["qkn_d"][l])
    qT, vT, k, qkv_c = _in_call(xs, mod, p["norm_attn_g"][l].reshape(1, -1), wt, wvc, gs)
    lambda_init = 0.8 - 0.6 * math.exp(-0.3 * l)
    o_a = _a_call(qT, k, vT, tbl_a, l)
    o_b = _b_call(p["sink_b"][l], qT, k, vT, tbl_b)
    o_c, lse_c = [], []
    for g, (_, r) in enumerate(DIL_PAIRS):
        o, lse = _c_call(qkv_c, tbl_c[g], g, r)
        o_c.append(o)
        lse_c.append(lse)
    o_d = _d_call(_d_score_bound(p["qkn_d"][l], p["t5_table"]), p["lam_d"][l],
                  p["subln_d"][l].reshape(HEAD_DIM, 1), qT, k, vT, tbl_d, lambda_init)
    x = _out_call(xs, mod, o_a, o_b, o_d, o_c, lse_c, p["w_out"][l].astype(BF16))
    wup, cw, wdn = _ffn_weights(p["w_up"][l], p["conv_w"][l], p["conv_b"][l], p["w_down"][l])
    return _ffn_call(x, mod, p["norm_ffn_g"][l].reshape(1, -1), wup, cw, wdn, split_out)


def _trunk(xs, c, p, depth=DEPTH):
    bt, d = c.shape
    mod_all = _ada_call(c, p["w_ada"], p["b_ada"]).reshape(DEPTH, bt, 6, d)
    tables = (_bias_tables_a(p["rpb_a"]),) + _bias_tables_t5(p["t5_table"])
    split = xs[0].shape[0] if len(xs) == 2 else None
    x = list(xs)
    for l in range(depth):
        x = _layer(x, mod_all[l], l, p, tables, split if l == depth - 1 else None)
        x = list(x) if isinstance(x, (list, tuple)) else [x]
    return tuple(x)


def kernel(x_prompt, x_sample, c_prompt, c_sample, norm_attn_g, norm_ffn_g, w_ada, b_ada, w_in, qkn_a, qkn_b, qkn_c, qkn_d, rpb_a, sink_b, t5_table, lam_d, subln_d, w_out, w_up, conv_w, conv_b, w_down):
    p = dict(norm_attn_g=norm_attn_g, norm_ffn_g=norm_ffn_g, w_ada=w_ada, b_ada=b_ada, w_in=w_in,
             qkn_a=qkn_a, qkn_b=qkn_b, qkn_c=qkn_c, qkn_d=qkn_d, rpb_a=rpb_a, sink_b=sink_b,
             t5_table=t5_table, lam_d=lam_d, subln_d=subln_d, w_out=w_out, w_up=w_up,
             conv_w=conv_w, conv_b=conv_b, w_down=w_down)
    c = jnp.concatenate([c_prompt, c_sample], axis=0)
    return _trunk((x_prompt, x_sample), c, p)
```

```python
import functools
import math

import numpy as np
import jax
import jax.numpy as jnp
from jax import lax
from jax.experimental import pallas as pl
from jax.experimental.pallas import tpu as pltpu

D_MODEL = 1024
SEQ = 2048
DEPTH = 4
HEAD_DIM = 64
GRID_W = 64
NA_HEADS = 4
NA_WIN_H = 8
NA_WIN_W = 16
SW_HEADS = 4
SW_KV_HEADS = 2
SW_HALF_WINDOW = 128
DIL_PAIRS = ((128, 1), (512, 4), (2048, 16))
DIFF_HEADS = 4
DIFF_HALF = HEAD_DIM // 2
T5_BUCKETS = 32
T5_MAX_DIST = 128
T5_B0 = 0
T5_C0 = 4
T5_D0 = 10
D_FF = 2816
RMS_EPS = 1e-6
NEG_INF = -1e30
LOG2E = 1.4426950408889634

F32 = jnp.float32
BF16 = jnp.bfloat16

VMEM_LIMIT_BYTES = 56 * 1024 * 1024
TN_IN = 512
TN_OUT = 512
TN_FFN = 512
FF_CHUNK = 256
N_FF_CHUNKS = D_FF // FF_CHUNK
HALO = 8
TQ_D = 512
D_SAFE_BOUND = 60.0
KB_D = 256
D_NEAR = TQ_D // KB_D + 2
TQ_B = 128
TQ_C = 128
A_PAIR = 2 * GRID_W
A_WIN = 10 * GRID_W

QT_ROWS = 768
VT_ROWS = 640
K_ROWS = 640
C_QK_ROWS = 768
WT_ROWS = QT_ROWS + VT_ROWS + K_ROWS + C_QK_ROWS
C_WIDTH = 1152
C_LANES = 2 * HEAD_DIM


def _cparams(sem):
    return pltpu.CompilerParams(dimension_semantics=sem, vmem_limit_bytes=VMEM_LIMIT_BYTES)


def _const_spec(shape):
    nd = len(shape)
    return pl.BlockSpec(shape, lambda *_: (0,) * nd, pipeline_mode=pl.Buffered(1))


def _ada_kernel(c_ref, w_ref, b_ref, o_ref):
    c = c_ref[...]
    sc = (c * (1.0 / (1.0 + jnp.exp(-c)))).astype(BF16)
    o_ref[0] = jnp.dot(sc, w_ref[0].astype(BF16), preferred_element_type=F32) + b_ref[0]


def _ada_call(c, w_ada, b_ada):
    bt = c.shape[0]
    ncol = 6 * D_MODEL
    cb = 1024
    return pl.pallas_call(
        _ada_kernel,
        grid=(DEPTH, ncol // cb),
        in_specs=[
            pl.BlockSpec((bt, D_MODEL), lambda l, j: (0, 0)),
            pl.BlockSpec((1, D_MODEL, cb), lambda l, j: (l, 0, j)),
            pl.BlockSpec((1, 1, cb), lambda l, j: (l, 0, j)),
        ],
        out_specs=pl.BlockSpec((1, bt, cb), lambda l, j: (l, 0, j)),
        out_shape=jax.ShapeDtypeStruct((DEPTH, bt, ncol), F32),
        compiler_params=_cparams(("arbitrary", "arbitrary")),
        name="ada_mod",
    )(c, w_ada, b_ada.reshape(DEPTH, 1, ncol))


def _mod_norm(x, g, scale, shift):
    ms = jnp.mean(x * x, axis=-1, keepdims=True)
    return x * lax.rsqrt(ms + RMS_EPS) * (g * (1.0 + scale)) + shift


def _pick_group(refs, nb_first):
    if len(refs) == 1:
        return refs[0][0]
    return jnp.where(pl.program_id(0) < nb_first, refs[0][0], refs[1][0])


def _group_specs(xs, tn):
    d = xs[0].shape[-1]
    if len(xs) == 1:
        return [pl.BlockSpec((1, tn, d), lambda b, t: (b, t, 0))]
    nb = xs[0].shape[0]
    last = xs[0].shape[1] // tn - 1
    return [pl.BlockSpec((1, tn, d), lambda b, t: (jnp.minimum(b, nb - 1), jnp.where(b < nb, t, last), 0)),
            pl.BlockSpec((1, tn, d), lambda b, t: (jnp.maximum(b - nb, 0), jnp.where(b < nb, 0, t), 0))]


def _in_kernel(*refs, nsrc, nb_first):
    x_refs, (mod_ref, g_ref, wt_ref, wvc_ref, gs_ref, qT_ref, vT_ref, k_ref, c_ref, h_scr) = \
        refs[:nsrc], refs[nsrc:]
    tn = x_refs[0].shape[1]
    nsub = 2
    ts = tn // nsub
    x = _pick_group(x_refs, nb_first)

    def norm_sub(i):
        h = _mod_norm(x[i * ts:(i + 1) * ts], g_ref[...], mod_ref[0, 1:2, :], mod_ref[0, 0:1, :])
        h_scr[i * ts:(i + 1) * ts, :] = h.astype(BF16)

    def head_norm(y, r0, grp):
        rows = y.shape[0]
        y3 = y.reshape(rows // grp, grp, ts)
        ms = jnp.sum(y3 * y3, axis=1, keepdims=True) * (1.0 / grp)
        gs = gs_ref[r0:r0 + rows, :].reshape(rows // grp, grp, 1)
        return (y3 * lax.rsqrt(ms + RMS_EPS) * gs).reshape(rows, ts)

    items = [(r0, 256, grp, ("qT", r0)) for r0, grp in ((0, HEAD_DIM), (256, HEAD_DIM), (512, DIFF_HALF))]
    items += [(QT_ROWS + r0, rows, None, ("vT", r0)) for r0, rows in ((0, 256), (256, 256), (512, 128))]
    base = QT_ROWS + VT_ROWS
    items += [(base + r0, rows, grp, ("k", r0))
              for r0, rows, grp in ((0, 256, DIFF_HALF), (256, 256, HEAD_DIM), (512, 128, HEAD_DIM))]
    base += K_ROWS
    items += [(base + r0, 256, HEAD_DIM, ("c", r0)) for r0 in (0, 256, 512)]
    items += [(None, C_WIDTH - C_QK_ROWS, None, ("c", C_QK_ROWS))]

    def matmul(item, i):
        w0, rows = item[0], item[1]
        hb = h_scr[i * ts:(i + 1) * ts, :]
        if w0 is None:
            return jnp.dot(hb, wvc_ref[...], preferred_element_type=F32)
        return lax.dot_general(wt_ref[w0:w0 + rows, :], hb, (((1,), (1,)), ((), ())),
                               preferred_element_type=F32)

    def finish(item, i, y):
        w0, rows, grp, (dst, r0) = item
        tok = slice(i * ts, (i + 1) * ts)
        if grp is not None:
            y = head_norm(y, w0, grp)
        if dst == "qT":
            qT_ref[0, r0:r0 + rows, tok] = y.astype(BF16)
        elif dst == "vT":
            vT_ref[0, r0:r0 + rows, tok] = y.astype(BF16)
        elif dst == "k":
            k_ref[0, tok, r0:r0 + rows] = y.T.astype(BF16)
        else:
            yt = y if w0 is None else y.T
            for j in range(rows // C_LANES):
                c_ref[r0 // C_LANES + j, 0, tok, :] = yt[:, j * C_LANES:(j + 1) * C_LANES].astype(BF16)

    work = [(item, i) for i in range(nsub) for item in items]
    norm_sub(0)
    pending = None
    for n_issued, (item, i) in enumerate(work):
        if n_issued == 2:
            for j in range(1, nsub):
                norm_sub(j)
        y = matmul(item, i)
        if pending is not None:
            finish(*pending)
        pending = (item, i, y)
    finish(*pending)


def _in_call(xs, mod, g, wt, wvc, gs):
    bt = sum(x.shape[0] for x in xs)
    _, n, d = xs[0].shape
    tn = TN_IN
    return pl.pallas_call(
        functools.partial(_in_kernel, nsrc=len(xs), nb_first=xs[0].shape[0]),
        grid=(bt, n // tn),
        in_specs=_group_specs(xs, tn) + [
            pl.BlockSpec((1, 6, d), lambda b, t: (b, 0, 0)),
            _const_spec((1, d)),
            _const_spec((WT_ROWS, d)),
            _const_spec((d, C_WIDTH - C_QK_ROWS)),
            _const_spec((WT_ROWS, 1)),
        ],
        out_specs=[
            pl.BlockSpec((1, QT_ROWS, tn), lambda b, t: (b, 0, t)),
            pl.BlockSpec((1, VT_ROWS, tn), lambda b, t: (b, 0, t)),
            pl.BlockSpec((1, tn, K_ROWS), lambda b, t: (b, t, 0)),
            pl.BlockSpec((C_WIDTH // C_LANES, 1, tn, C_LANES), lambda b, t: (0, b, t, 0)),
        ],
        out_shape=[
            jax.ShapeDtypeStruct((bt, QT_ROWS, n), BF16),
            jax.ShapeDtypeStruct((bt, VT_ROWS, n), BF16),
            jax.ShapeDtypeStruct((bt, n, K_ROWS), BF16),
            jax.ShapeDtypeStruct((C_WIDTH // C_LANES, bt, n, C_LANES), BF16),
        ],
        scratch_shapes=[pltpu.VMEM((tn, d), BF16)],
        compiler_params=_cparams(("arbitrary", "arbitrary")),
        name="in_proj",
    )(*xs, mod, g, wt, wvc, gs)


def _d_kernel(bound_ref, lam_ref, sub_ref, q_ref, k_ref, v_ref, tbl_ref, o_ref, s_scr, p_scr, *, lambda_init):
    t = pl.program_id(1)
    tq = q_ref.shape[2]
    nkb = k_ref.shape[1] // KB_D
    qT = q_ref[0]
    rowgrp = lax.broadcasted_iota(jnp.int32, qT.shape, 0) // DIFF_HALF
    lp = lam_ref[...]
    lam = (jnp.exp(jnp.sum(lp[0:1] * lp[1:2], axis=-1, keepdims=True))
           - jnp.exp(jnp.sum(lp[2:3] * lp[3:4], axis=-1, keepdims=True)) + lambda_init)

    first = t * (tq // KB_D)
    blocks = []
    for d in range(-1, nkb - 1):
        jb = lax.rem(first + d + nkb, nkb)
        blocks.append((d < D_NEAR - 1, pl.multiple_of(jb * KB_D, KB_D),
                       jnp.clip(jb - first, -2, D_NEAR - 1) + 2))

    nblk = len(blocks)
    vmax = lambda s: jnp.max(jnp.max(s.reshape(KB_D // 8, 8, tq), axis=0), axis=0, keepdims=True)

    def scores(h, pos, st):
        near, k0, idx = blocks[pos]
        kb = k_ref[0, pl.ds(k0, KB_D), :]
        for i in range(2):
            slot = 2 * (h % 2) + i
            s = jnp.dot(kb, st["q"][i], preferred_element_type=F32)
            if near:
                s = s + tbl_ref[h, idx]
                s_scr[slot, pos] = s
                st["m"][i] = jnp.maximum(st["m"][i], vmax(s))
            else:
                s_scr[slot, pos] = s
                st["m"][i] = jnp.maximum(st["m"][i], vmax(s) + tbl_ref[h, idx, 0:1, :])

    def probs(h, pos, st):
        near, k0, idx = blocks[pos]
        for i in range(2):
            slot = 2 * (h % 2) + i
            shift = st["m"][i] if near else st["m"][i] - tbl_ref[h, idx, 0:1, :]
            p_scr[slot, pl.ds(k0, KB_D), :] = jnp.exp2(s_scr[slot, pos] - shift).astype(BF16)

    def new_state(h):
        return {"q": [jnp.where(rowgrp == 2 * h + i, qT, jnp.zeros_like(qT)) for i in range(2)],
                "m": [jnp.full((1, tq), -jnp.inf, F32)] * 2}

    ones = jnp.ones((16, k_ref.shape[1]), BF16)

    def finish(h, st):
        vh = jnp.concatenate([v_ref[0, h * HEAD_DIM:(h + 1) * HEAD_DIM, :], ones], axis=0)
        o = []
        for i in range(2):
            ol = jnp.dot(vh, p_scr[2 * (h % 2) + i], preferred_element_type=F32)
            o.append(ol[:HEAD_DIM] * (1.0 / ol[HEAD_DIM:HEAD_DIM + 1]))
        oh = o[0] - lam * o[1]
        ms = jnp.mean(oh * oh, axis=0, keepdims=True)
        return oh * lax.rsqrt(ms + RMS_EPS) * sub_ref[...] * (1.0 - lambda_init)

    def exact_max_path():
        st = new_state(0)
        for pos in range(nblk):
            scores(0, pos, st)
        parts = []
        for h in range(DIFF_HEADS):
            nxt = new_state(h + 1) if h + 1 < DIFF_HEADS else None
            for pos in range(nblk):
                if nxt is not None:
                    scores(h + 1, pos, nxt)
                probs(h, pos, st)
            parts.append(finish(h, st))
            st = nxt
        o_ref[0] = jnp.concatenate(parts, axis=0).T.astype(BF16)

    def bounded_path():
        bound = bound_ref[0]

        def block(h, st, pos):
            near, k0, idx = blocks[pos]
            kb = k_ref[0, pl.ds(k0, KB_D), :]
            for i in range(2):
                s = jnp.dot(kb, st["q"][i], preferred_element_type=F32)
                if near:
                    p = jnp.exp2(s + tbl_ref[h, idx] - bound)
                else:
                    p = jnp.exp2(s - (bound - tbl_ref[h, idx, 0:1, :]))
                p_scr[2 * (h % 2) + i, pl.ds(k0, KB_D), :] = p.astype(BF16)

        lag = 3
        parts = []
        states = [new_state(h) for h in range(DIFF_HEADS)]
        for h in range(DIFF_HEADS):
            for pos in range(nblk):
                block(h, states[h], pos)
                if h > 0 and pos == lag - 1:
                    parts.append(finish(h - 1, states[h - 1]))
        parts.append(finish(DIFF_HEADS - 1, states[-1]))
        o_ref[0] = jnp.concatenate(parts, axis=0).T.astype(BF16)

    pl.when(bound_ref[0] <= D_SAFE_BOUND)(bounded_path)
    pl.when(bound_ref[0] > D_SAFE_BOUND)(exact_max_path)


def _d_score_bound(qkn_d_l, t5_table):
    gq = jnp.max(jnp.abs(qkn_d_l[0])) * (DIFF_HALF ** -0.5 * LOG2E)
    gk = jnp.max(jnp.abs(qkn_d_l[1]))
    bias = jnp.max(jnp.abs(t5_table[:, T5_D0:T5_D0 + DIFF_HEADS])) * LOG2E
    return (1.02 * DIFF_HALF * gq * gk + bias).astype(F32).reshape(1)


def _d_call(bound, lam_p, subln, qT, k, vT, tbl, lambda_init):
    bt, _, n = qT.shape
    tq = TQ_D
    return pl.pallas_call(
        functools.partial(_d_kernel, lambda_init=lambda_init),
        grid=(bt, n // tq),
        in_specs=[
            pl.BlockSpec(memory_space=pltpu.SMEM),
            _const_spec((4, DIFF_HALF)),
            _const_spec((HEAD_DIM, 1)),
            pl.BlockSpec((1, 256, tq), lambda b, t: (b, 2, t)),
            pl.BlockSpec((1, n, 256), lambda b, t: (b, 0, 0)),
            pl.BlockSpec((1, 256, n), lambda b, t: (b, 0, 0)),
            _const_spec((DIFF_HEADS, D_NEAR + 2, KB_D, tq)),
        ],
        out_specs=pl.BlockSpec((1, tq, 256), lambda b, t: (b, t, 0)),
        out_shape=jax.ShapeDtypeStruct((bt, n, 256), BF16),
        scratch_shapes=[
            pltpu.VMEM((4, n // KB_D, KB_D, tq), F32),
            pltpu.VMEM((4, n, tq), BF16),
        ],
        compiler_params=_cparams(("parallel", "arbitrary")),
        name="mixer_d",
    )(bound, lam_p, subln, qT, k, vT, tbl)


def _b_kernel(sink_ref, q_ref, k_ref, v_ref, bias_ref, o_ref):
    n = k_ref.shape[1]
    tq = TQ_B
    nt = n // tq
    grp = SW_HEADS // SW_KV_HEADS
    zero = jnp.zeros((HEAD_DIM, tq), BF16)
    sinks = [sink_ref[h] * LOG2E for h in range(SW_HEADS)]
    group = 4
    for t0 in range(0, nt, group):
        chains = []
        for t in range(t0, t0 + group):
            qT = q_ref[0, :, t * tq:(t + 1) * tq]
            qcat = []
            for h in range(SW_HEADS):
                qh = qT[h * HEAD_DIM:(h + 1) * HEAD_DIM]
                qcat.append(jnp.concatenate([qh, zero] if h // grp == 0 else [zero, qh], axis=0))
            qcat = jnp.concatenate(qcat, axis=1)
            blocks = [j for j in (-1, 0, 1) if 0 <= t + j < nt]
            s_all = [jnp.dot(k_ref[0, (t + j) * tq:(t + j + 1) * tq, :], qcat,
                             preferred_element_type=F32) for j in blocks]
            for h in range(SW_HEADS):
                chains.append((t, h, [(s[:, h * tq:(h + 1) * tq] + bias_ref[h, j + 1], t + j)
                                      for s, j in zip(s_all, blocks)]))
        m_l = []
        for t, h, ss in chains:
            m = jnp.full((1, tq), sinks[h], F32)
            for s, _ in ss:
                m = jnp.maximum(m, jnp.max(s, axis=0, keepdims=True))
            m_l.append(m)
        p_l = [[jnp.exp2(s - m) for s, _ in ss] for (t, h, ss), m in zip(chains, m_l)]
        outs = []
        for (t, h, ss), m, ps in zip(chains, m_l, p_l):
            kv = h // grp
            l = jnp.exp2(sinks[h] - m)
            o = jnp.zeros((HEAD_DIM, tq), F32)
            for p, (_, kb) in zip(ps, ss):
                l = l + jnp.sum(p, axis=0, keepdims=True)
                vh = v_ref[0, kv * HEAD_DIM:(kv + 1) * HEAD_DIM, kb * tq:(kb + 1) * tq]
                o = o + jnp.dot(vh, p.astype(BF16), preferred_element_type=F32)
            outs.append(o * (1.0 / l))
        for i, t in enumerate(range(t0, t0 + group)):
            o_ref[0, t * tq:(t + 1) * tq, :] = jnp.concatenate(
                outs[i * SW_HEADS:(i + 1) * SW_HEADS], axis=0).T.astype(BF16)


def _b_call(sink, qT, k, vT, bias):
    bt, _, n = qT.shape
    return pl.pallas_call(
        _b_kernel,
        grid=(bt,),
        in_specs=[
            pl.BlockSpec(memory_space=pltpu.SMEM),
            pl.BlockSpec((1, 256, n), lambda b: (b, 1, 0)),
            pl.BlockSpec((1, n, 128), lambda b: (b, 0, 4)),
            pl.BlockSpec((1, 128, n), lambda b: (b, 4, 0)),
            _const_spec((SW_HEADS, 3, TQ_B, TQ_B)),
        ],
        out_specs=pl.BlockSpec((1, n, 256), lambda b: (b, 0, 0)),
        out_shape=jax.ShapeDtypeStruct((bt, n, 256), BF16),
        compiler_params=_cparams(("parallel",)),
        name="mixer_b",
    )(sink, qT, k, vT, bias)


def _a_kernel(q_ref, k_ref, v_ref, bias_ref, o_ref):
    n = k_ref.shape[1]
    rows = n // GRID_W
    npair = n // A_PAIR
    kh = min(NA_WIN_H, rows)
    rowhead = lax.broadcasted_iota(jnp.int32, (NA_HEADS * HEAD_DIM, A_PAIR), 0) // HEAD_DIM
    group = 8
    for g0 in range(0, npair, group):
        geo, s_all = [], []
        for rp in range(g0, g0 + group):
            r = 2 * rp
            ws = min(max(r - kh // 2, 0), rows - kh, rows - A_WIN // GRID_W)
            var = rp if rp < 2 else (2 if rp < npair - 2 else rp - (npair - 5))
            q0, k0 = rp * A_PAIR, ws * GRID_W
            qT = q_ref[0, :, q0:q0 + A_PAIR]
            qcat = jnp.concatenate([jnp.where(rowhead == h, qT, jnp.zeros_like(qT))
                                    for h in range(NA_HEADS)], axis=1)
            s_all.append(jnp.dot(k_ref[0, k0:k0 + A_WIN, :], qcat, preferred_element_type=F32))
            geo.append((q0, k0, var))
        chains = [(i, h) for i in range(group) for h in range(NA_HEADS)]
        s_l = [s_all[i][:, h * A_PAIR:(h + 1) * A_PAIR] + bias_ref[h, geo[i][2]] for i, h in chains]
        m_l = [jnp.max(s, axis=0, keepdims=True) for s in s_l]
        p_l = [jnp.exp2(s - m) for s, m in zip(s_l, m_l)]
        l_l = [jnp.sum(p, axis=0, keepdims=True) for p in p_l]
        o_l = [jnp.dot(v_ref[0, h * HEAD_DIM:(h + 1) * HEAD_DIM, geo[i][1]:geo[i][1] + A_WIN],
                       p.astype(BF16), preferred_element_type=F32) * (1.0 / l)
               for (i, h), p, l in zip(chains, p_l, l_l)]
        for i in range(group):
            q0 = geo[i][0]
            o_ref[0, q0:q0 + A_PAIR, :] = jnp.concatenate(
                o_l[i * NA_HEADS:(i + 1) * NA_HEADS], axis=0).T.astype(BF16)


def _a_call(qT, k, vT, bias, layer):
    bt, _, n = qT.shape
    return pl.pallas_call(
        _a_kernel,
        grid=(bt,),
        in_specs=[
            pl.BlockSpec((1, 256, n), lambda b: (b, 0, 0)),
            pl.BlockSpec((1, n, 256), lambda b: (b, 0, 1)),
            pl.BlockSpec((1, 256, n), lambda b: (b, 1, 0)),
            pl.BlockSpec((NA_HEADS, 5, A_WIN, A_PAIR), lambda b: (layer, 0, 0, 0),
                         pipeline_mode=pl.Buffered(1)),
        ],
        out_specs=pl.BlockSpec((1, n, 256), lambda b: (b, 0, 0)),
        out_shape=jax.ShapeDtypeStruct((bt, n, 256), BF16),
        compiler_params=_cparams(("parallel",)),
        name="mixer_a",
    )(qT, k, vT, bias)


def _c_kernel(q_ref, k_ref, v_ref, bias_ref, o_ref, lse_ref, *, dil):
    sub_len = q_ref.shape[2]
    n = sub_len * dil
    win = min(2 * TQ_C, sub_len)
    nqb = sub_len // TQ_C
    lane_head = lax.broadcasted_iota(jnp.int32, (TQ_C, 2 * HEAD_DIM), 1) // HEAD_DIM

    group = 4
    for g0 in range(0, n // TQ_C, group):
        geo = []
        for item in range(g0, g0 + group):
            p, qb = divmod(item, nqb)
            start = min(max(qb * TQ_C - TQ_C // 2, 0), sub_len - win)
            var = 0 if qb == 0 else (2 if qb == nqb - 1 else 1)
            geo.extend((p, qb * TQ_C, start, var, j) for j in range(2))
        s_l = []
        for p, q0, start, var, j in geo:
            lanes = slice(p * C_LANES, (p + 1) * C_LANES)
            qblk = q_ref[0, 0, q0:q0 + TQ_C, lanes]
            qm = jnp.where(lane_head == j, qblk, jnp.zeros_like(qblk))
            s = lax.dot_general(qm, k_ref[0, 0, start:start + win, lanes], (((1,), (1,)), ((), ())),
                                preferred_element_type=F32)
            s_l.append(s + bias_ref[j, var])
        m_l = [jnp.max(s, axis=-1, keepdims=True) for s in s_l]
        p_l = [jnp.exp2(s - m) for s, m in zip(s_l, m_l)]
        l_l = [jnp.sum(pr, axis=-1, keepdims=True) for pr in p_l]
        o_l = [jnp.dot(pr.astype(BF16), v_ref[0, 0, start:start + win, p * C_LANES:(p + 1) * C_LANES],
                       preferred_element_type=F32)
               for pr, (p, q0, start, var, j) in zip(p_l, geo)]
        for i in range(0, len(geo), 2):
            p, q0 = geo[i][0], geo[i][1]
            lanes = slice(p * C_LANES, (p + 1) * C_LANES)
            o0, o1 = o_l[i] * (1.0 / l_l[i]), o_l[i + 1] * (1.0 / l_l[i + 1])
            e0 = jnp.broadcast_to(m_l[i] + jnp.log2(l_l[i]), (TQ_C, 2 * HEAD_DIM))
            e1 = jnp.broadcast_to(m_l[i + 1] + jnp.log2(l_l[i + 1]), (TQ_C, 2 * HEAD_DIM))
            o_ref[0, q0:q0 + TQ_C, lanes] = jnp.where(lane_head == 0, o0, o1)
            lse_ref[0, q0:q0 + TQ_C, lanes] = jnp.where(lane_head == 0, e0, e1)


def _c_call(qkv, bias, g, dil):
    _, bt, n, lanes = qkv.shape
    sub_len = n // dil
    win = min(2 * TQ_C, sub_len)
    view = qkv.reshape(qkv.shape[0], bt, sub_len, dil * lanes)
    blk = lambda j: pl.BlockSpec((1, 1, sub_len, dil * lanes), lambda b: (j, b, 0, 0))
    o, lse = pl.pallas_call(
        functools.partial(_c_kernel, dil=dil),
        grid=(bt,),
        in_specs=[blk(g), blk(3 + g), blk(6 + g), _const_spec((2, 3, TQ_C, win))],
        out_specs=[
            pl.BlockSpec((1, sub_len, dil * lanes), lambda b: (b, 0, 0)),
            pl.BlockSpec((1, sub_len, dil * lanes), lambda b: (b, 0, 0)),
        ],
        out_shape=[
            jax.ShapeDtypeStruct((bt, sub_len, dil * lanes), F32),
            jax.ShapeDtypeStruct((bt, sub_len, dil * lanes), F32),
        ],
        compiler_params=_cparams(("parallel",)),
        name=f"mixer_c{g}",
    )(view, view, view, bias)
    return o.reshape(bt, n, lanes), lse.reshape(bt, n, lanes)


def _out_kernel(*refs, nsrc, nb_first):
    x_refs, (mod_ref, oa_ref, ob_ref, od_ref, oc0_ref, oc1_ref, oc2_ref,
             l0_ref, l1_ref, l2_ref, w_ref, o_ref) = refs[:nsrc], refs[nsrc:]
    l0, l1, l2 = l0_ref[0], l1_ref[0], l2_ref[0]
    mx = jnp.maximum(jnp.maximum(l0, l1), l2)
    e0, e1, e2 = jnp.exp2(l0 - mx), jnp.exp2(l1 - mx), jnp.exp2(l2 - mx)
    inv = 1.0 / (e0 + e1 + e2)
    mixed = jnp.concatenate([oa_ref[0], ob_ref[0],
                             (oc0_ref[0] * (e0 * inv)).astype(BF16),
                             (oc1_ref[0] * (e1 * inv)).astype(BF16),
                             (oc2_ref[0] * (e2 * inv)).astype(BF16), od_ref[0]], axis=1)
    y = jnp.dot(mixed, w_ref[...], preferred_element_type=F32)
    o_ref[0] = _pick_group(x_refs, nb_first) + mod_ref[0, 2:3, :] * y


def _out_call(xs, mod, o_a, o_b, o_d, o_c, lse_c, w_out):
    bt = sum(x.shape[0] for x in xs)
    _, n, d = xs[0].shape
    tn = TN_OUT
    tok = lambda w: pl.BlockSpec((1, tn, w), lambda b, t: (b, t, 0))
    return pl.pallas_call(
        functools.partial(_out_kernel, nsrc=len(xs), nb_first=xs[0].shape[0]),
        grid=(bt, n // tn),
        in_specs=_group_specs(xs, tn) + [pl.BlockSpec((1, 6, d), lambda b, t: (b, 0, 0)),
                  tok(256), tok(256), tok(256), tok(128), tok(128), tok(128),
                  tok(128), tok(128), tok(128), _const_spec(w_out.shape)],
        out_specs=tok(d),
        out_shape=jax.ShapeDtypeStruct((bt, n, d), F32),
        compiler_params=_cparams(("arbitrary", "arbitrary")),
        name="out_proj",
    )(*xs, mod, o_a, o_b, o_d, *o_c, *lse_c, w_out)


def _ffn_kernel(xp_ref, x_ref, xn_ref, mod_ref, g_ref, wup_ref, cw_ref, wdn_ref, *refs, nb_first):
    o_refs, (h_scr, acc_scr, act_scr) = refs[:-3], refs[-3:]
    t = pl.program_id(1)
    nt = pl.num_programs(1)
    tn = x_ref.shape[1]
    ext = tn + 2 * HALO
    xe = jnp.concatenate([xp_ref[0], x_ref[0], xn_ref[0]], axis=0)
    h = _mod_norm(xe, g_ref[...], mod_ref[0, 4:5, :], mod_ref[0, 3:4, :])
    row = lax.broadcasted_iota(jnp.int32, (ext, 1), 0)
    keep = jnp.logical_and(jnp.logical_or(row >= HALO, t > 0),
                           jnp.logical_or(row < HALO + tn, t < nt - 1))
    h_scr[...] = jnp.where(keep, h, 0.0).astype(BF16)

    nsplit = 4
    rows = tn // nsplit

    def conv(u, cw, r0):
        ue = u[r0:r0 + rows + 2 * HALO]
        prev = pltpu.roll(ue, 1, 0)[HALO:HALO + rows]
        nxt = pltpu.roll(ue, rows + 2 * HALO - 1, 0)[HALO:HALO + rows]
        return prev * cw[0:1] + ue[HALO:HALO + rows] * cw[1:2] + nxt * cw[2:3] + cw[3:4]

    hb = h_scr[...]
    up = lambda c: jnp.dot(hb, wup_ref[c], preferred_element_type=F32)
    u_val, u_gate = up(0), up(N_FF_CHUNKS)
    kgroup = 6
    for c in range(N_FF_CHUNKS):
        nxt_u = []
        g0 = c - c % kgroup
        gl = min(kgroup, N_FF_CHUNKS - g0)
        last_of_group = c == g0 + gl - 1
        col = (c - g0) * FF_CHUNK
        for part in range(nsplit):
            r0 = part * rows
            if c + 1 < N_FF_CHUNKS and part % (nsplit // 2) == 0:
                nxt_u.append(up(c + 1 + len(nxt_u) * N_FF_CHUNKS))
            val = conv(u_val, cw_ref[c], r0)
            gate = conv(u_gate, cw_ref[N_FF_CHUNKS + c], r0)
            act_scr[r0:r0 + rows, col:col + FF_CHUNK] = (val * gate * jax.nn.sigmoid(gate)).astype(BF16)
            if last_of_group:
                wd = wdn_ref[g0:g0 + gl].reshape(gl * FF_CHUNK, wdn_ref.shape[2])
                y = jnp.dot(act_scr[r0:r0 + rows, 0:gl * FF_CHUNK], wd, preferred_element_type=F32)
                if g0 == 0:
                    acc_scr[r0:r0 + rows, :] = y
                else:
                    acc_scr[r0:r0 + rows, :] += y
        if nxt_u:
            u_val, u_gate = nxt_u
    if len(o_refs) == 1:
        o_refs[0][0] = x_ref[0] + mod_ref[0, 5:6, :] * acc_scr[...]
    else:
        first = pl.program_id(0) < nb_first

        @pl.when(first)
        def _():
            o_refs[0][0] = x_ref[0] + mod_ref[0, 5:6, :] * acc_scr[...]

        @pl.when(jnp.logical_not(first))
        def _():
            o_refs[1][0] = x_ref[0] + mod_ref[0, 5:6, :] * acc_scr[...]


def _ffn_call(x, mod, g, wup, cw, wdn, nb_first=None):
    bt, n, d = x.shape
    tn = TN_FFN
    nb = tn // HALO
    nhalo = n // HALO
    if nb_first is None:
        out_specs = pl.BlockSpec((1, tn, d), lambda b, t: (b, t, 0))
        out_shape = jax.ShapeDtypeStruct((bt, n, d), F32)
    else:
        last = n // tn - 1
        out_specs = [
            pl.BlockSpec((1, tn, d), lambda b, t: (jnp.minimum(b, nb_first - 1),
                                                   jnp.where(b < nb_first, t, last), 0)),
            pl.BlockSpec((1, tn, d), lambda b, t: (jnp.maximum(b - nb_first, 0),
                                                   jnp.where(b < nb_first, 0, t), 0))]
        out_shape = [jax.ShapeDtypeStruct((nb_first, n, d), F32),
                     jax.ShapeDtypeStruct((bt - nb_first, n, d), F32)]
    return pl.pallas_call(
        functools.partial(_ffn_kernel, nb_first=nb_first),
        grid=(bt, n // tn),
        in_specs=[
            pl.BlockSpec((1, HALO, d), lambda b, t: (b, jnp.maximum(t * nb - 1, 0), 0)),
            pl.BlockSpec((1, tn, d), lambda b, t: (b, t, 0)),
            pl.BlockSpec((1, HALO, d), lambda b, t: (b, jnp.minimum((t + 1) * nb, nhalo - 1), 0)),
            pl.BlockSpec((1, 6, d), lambda b, t: (b, 0, 0)),
            _const_spec((1, d)),
            _const_spec(wup.shape),
            _const_spec(cw.shape),
            _const_spec(wdn.shape),
        ],
        out_specs=out_specs,
        out_shape=out_shape,
        scratch_shapes=[pltpu.VMEM((tn + 2 * HALO, d), BF16), pltpu.VMEM((tn, d), F32),
                        pltpu.VMEM((tn, 6 * FF_CHUNK), BF16)],
        compiler_params=_cparams(("arbitrary", "arbitrary")),
        name="ffn",
    )(x, x, x, mod, g, wup, cw, wdn)


def _t5_bucket(rel):
    nb = T5_BUCKETS // 2
    max_exact = nb // 2
    base = jnp.where(rel > 0, nb, 0)
    n = jnp.abs(rel)
    nf = jnp.maximum(n, 1).astype(jnp.float32)
    large = max_exact + (jnp.log(nf / max_exact) / math.log(T5_MAX_DIST / max_exact)
                         * (nb - max_exact)).astype(jnp.int32)
    large = jnp.minimum(large, nb - 1)
    return base + jnp.where(n < max_exact, n, large)


def _t5_kernel(tab_ref, bkt_ref, o_ref, *, head0):
    h = head0 + pl.program_id(0)
    for v in range(bkt_ref.shape[0]):
        bkt = bkt_ref[v]
        acc = jnp.full(bkt.shape, NEG_INF, F32)
        for b in range(T5_BUCKETS):
            acc = jnp.where(bkt == b, tab_ref[b, h] * LOG2E, acc)
        o_ref[0, v] = acc


def _t5_bias(table, rel, mask, head0, nheads):
    bkt = jnp.where(jnp.asarray(mask), _t5_bucket(jnp.asarray(rel, dtype=jnp.int32)), -1)
    return pl.pallas_call(
        functools.partial(_t5_kernel, head0=head0),
        grid=(nheads,),
        in_specs=[pl.BlockSpec(memory_space=pltpu.SMEM), _const_spec(bkt.shape)],
        out_specs=pl.BlockSpec((1,) + bkt.shape, lambda h: (h, 0, 0, 0)),
        out_shape=jax.ShapeDtypeStruct((nheads,) + bkt.shape, F32),
        compiler_params=_cparams(("arbitrary",)),
        name="t5_bias",
    )(table.astype(F32), bkt)


def _bias_tables_t5(t5_table):
    kk = np.arange(KB_D)[:, None]
    qq = np.arange(TQ_D)[None, :]
    far = 8 * T5_MAX_DIST
    rel_d = np.stack([np.full((KB_D, TQ_D), -far)]
                     + [j * KB_D + kk - qq for j in range(-1, D_NEAR - 1)]
                     + [np.full((KB_D, TQ_D), far)])
    tbl_d = _t5_bias(t5_table, rel_d, np.ones_like(rel_d, bool), T5_D0, DIFF_HEADS)
    kk = np.arange(TQ_B)[:, None]
    qq = np.arange(TQ_B)[None, :]
    rel_b = np.stack([j * TQ_B + kk - qq for j in (-1, 0, 1)])
    tbl_b = _t5_bias(t5_table, rel_b, np.abs(rel_b) <= SW_HALF_WINDOW, T5_B0, SW_HEADS)
    tbl_c = []
    for g, (w, r) in enumerate(DIL_PAIRS):
        sub_len = SEQ // r
        hw = w // (2 * r)
        win = min(2 * TQ_C, sub_len)
        qq = np.arange(TQ_C)[:, None]
        kk = np.arange(win)[None, :]
        offs = (0, -(TQ_C // 2), -(win - TQ_C))
        rel_c = np.stack([o + kk - qq for o in offs])
        tbl_c.append(_t5_bias(t5_table, rel_c * r, np.abs(rel_c) <= hw, T5_C0 + 2 * g, 2))
    return tbl_d, tbl_b, tbl_c


def _a_variants():
    rows = SEQ // GRID_W
    kh = min(NA_WIN_H, rows)
    wrows = A_WIN // GRID_W
    out = []
    for r in (0, 2, 4, rows - 4, rows - 2):
        ws = min(int(np.clip(r - kh // 2, 0, rows - kh)), rows - wrows)
        per_row = []
        for kr in range(wrows):
            pair = []
            for a in range(2):
                rsq = int(np.clip(r + a - kh // 2, 0, rows - kh))
                krow = ws + kr
                pair.append(krow - (r + a) + NA_WIN_H - 1 if rsq <= krow < rsq + kh else None)
            per_row.append(tuple(pair))
        out.append(tuple(per_row))
    return tuple(out)


def _abias_kernel(rpb_ref, dcm_ref, o_ref):
    i = pl.program_id(0)
    ncol = 2 * NA_WIN_W - 1
    dcm = dcm_ref[...]
    second = lax.broadcasted_iota(jnp.int32, dcm.shape, 1) >= GRID_W
    neg = jnp.full(dcm.shape, NEG_INF, F32)
    where = {}
    for v, per_row in enumerate(_a_variants()):
        for kr, pair in enumerate(per_row):
            where.setdefault(pair, []).append((v, kr))
    for (dr0, dr1), places in where.items():
        if dr0 is None and dr1 is None:
            tile = neg
        else:
            def body(t, acc, dr0=dr0, dr1=dr1):
                s0 = NEG_INF if dr0 is None else rpb_ref[i, dr0 * ncol + t] * LOG2E
                s1 = NEG_INF if dr1 is None else rpb_ref[i, dr1 * ncol + t] * LOG2E
                return jnp.where(dcm == t, jnp.where(second, s1, s0), acc)

            tile = lax.fori_loop(0, ncol, body, neg)
        for v, kr in places:
            o_ref[0, v, kr * GRID_W:(kr + 1) * GRID_W, :] = tile


def _bias_tables_a(rpb_a):
    nl, nh, nr, nc = rpb_a.shape
    kc = np.arange(GRID_W)[:, None]
    c = np.tile(np.arange(GRID_W), 2)[None, :]
    cs = np.clip(c - NA_WIN_W // 2, 0, GRID_W - NA_WIN_W)
    dcm = np.where((kc >= cs) & (kc < cs + NA_WIN_W), np.clip(kc - c + NA_WIN_W - 1, 0, nc - 1), -1)
    return pl.pallas_call(
        _abias_kernel,
        grid=(nl * nh,),
        in_specs=[pl.BlockSpec(memory_space=pltpu.SMEM), _const_spec(dcm.shape)],
        out_specs=pl.BlockSpec((1, 5, A_WIN, A_PAIR), lambda i: (i, 0, 0, 0)),
        out_shape=jax.ShapeDtypeStruct((nl * nh, 5, A_WIN, A_PAIR), F32),
        compiler_params=_cparams(("arbitrary",)),
        name="rpb_bias",
    )(rpb_a.reshape(nl * nh, nr * nc).astype(F32), jnp.asarray(dcm, jnp.int32))


def _in_weights(w_in_l, qkn_a, qkn_b, qkn_c, qkn_d):
    a0, b0, c0, d0 = 0, 768, 1280, 2432
    segs = ((a0, 256), (b0, 256), (d0, 256),
            (d0 + 512, 256), (a0 + 512, 256), (b0 + 384, 128),
            (d0 + 256, 256), (a0 + 256, 256), (b0 + 256, 128),
            (c0, 768))
    wt = jnp.concatenate([w_in_l[:, s0:s0 + w] for s0, w in segs], axis=1).T.astype(BF16)
    wvc = w_in_l[:, c0 + 768:c0 + 1152].astype(BF16)
    qs64 = HEAD_DIM ** -0.5 * LOG2E
    qs32 = DIFF_HALF ** -0.5 * LOG2E
    ones = jnp.ones((VT_ROWS,), F32)
    gs = jnp.concatenate([
        jnp.tile(qkn_a[0] * qs64, 4), jnp.tile(qkn_b[0] * qs64, 4), jnp.tile(qkn_d[0] * qs32, 8),
        ones,
        jnp.tile(qkn_d[1], 8), jnp.tile(qkn_a[1], 4), jnp.tile(qkn_b[1], 2),
        jnp.tile(qkn_c[0] * qs64, 6), jnp.tile(qkn_c[1], 6),
    ]).astype(F32).reshape(WT_ROWS, 1)
    return wt, wvc, gs


def _ffn_weights(w_up_l, conv_w_l, conv_b_l, w_down_l):
    d = w_up_l.shape[0]
    wup = w_up_l.reshape(d, 2 * N_FF_CHUNKS, FF_CHUNK).transpose(1, 0, 2).astype(BF16)
    cw = jnp.concatenate([conv_w_l, conv_b_l[None], jnp.zeros((4, 2 * D_FF), F32)], axis=0)
    cw = cw.reshape(8, 2 * N_FF_CHUNKS, FF_CHUNK).transpose(1, 0, 2)
    wdn = w_down_l.reshape(N_FF_CHUNKS, FF_CHUNK, d).astype(BF16)
    return wup, cw, wdn


def _layer(xs, mod, l, p, tables, split_out=None):
    tbl_a, tbl_d, tbl_b, tbl_c = tables
    wt, wvc, gs = _in_weights(p["w_in"][l], p["qkn_a"][l], p["qkn_b"][l], p["qkn_c"][l], p["qkn_d"][l])
    qT, vT, k, qkv_c = _in_call(xs, mod, p["norm_attn_g"][l].reshape(1, -1), wt, wvc, gs)
    lambda_init = 0.8 - 0.6 * math.exp(-0.3 * l)
    o_a = _a_call(qT, k, vT, tbl_a, l)
    o_b = _b_call(p["sink_b"][l], qT, k, vT, tbl_b)
    o_c, lse_c = [], []
    for g, (_, r) in enumerate(DIL_PAIRS):
        o, lse = _c_call(qkv_c, tbl_c[g], g, r)
        o_c.append(o)
        lse_c.append(lse)
    o_d = _d_call(_d_score_bound(p["qkn_d"][l], p["t5_table"]), p["lam_d"][l],
                  p["subln_d"][l].reshape(HEAD_DIM, 1), qT, k, vT, tbl_d, lambda_init)
    x = _out_call(xs, mod, o_a, o_b, o_d, o_c, lse_c, p["w_out"][l].astype(BF16))
    wup, cw, wdn = _ffn_weights(p["w_up"][l], p["conv_w"][l], p["conv_b"][l], p["w_down"][l])
    return _ffn_call(x, mod, p["norm_ffn_g"][l].reshape(1, -1), wup, cw, wdn, split_out)


def _trunk(xs, c, p, depth=DEPTH):
    bt, d = c.shape
    mod_all = _ada_call(c, p["w_ada"], p["b_ada"]).reshape(DEPTH, bt, 6, d)
    tables = (_bias_tables_a(p["rpb_a"]),) + _bias_tables_t5(p["t5_table"])
    split = xs[0].shape[0] if len(xs) == 2 else None
    x = list(xs)
    for l in range(depth):
        x = _layer(x, mod_all[l], l, p, tables, split if l == depth - 1 else None)
        x = list(x) if isinstance(x, (list, tuple)) else [x]
    return tuple(x)


def kernel(x_prompt, x_sample, c_prompt, c_sample, norm_attn_g, norm_ffn_g, w_ada, b_ada, w_in, qkn_a, qkn_b, qkn_c, qkn_d, rpb_a, sink_b, t5_table, lam_d, subln_d, w_out, w_up, conv_w, conv_b, w_down):
    p = dict(norm_attn_g=norm_attn_g, norm_ffn_g=norm_ffn_g, w_ada=w_ada, b_ada=b_ada, w_in=w_in,
             qkn_a=qkn_a, qkn_b=qkn_b, qkn_c=qkn_c, qkn_d=qkn_d, rpb_a=rpb_a, sink_b=sink_b,
             t5_table=t5_table, lam_d=lam_d, subln_d=subln_d, w_out=w_out, w_up=w_up,
             conv_w=conv_w, conv_b=conv_b, w_down=w_down)
    c = jnp.concatenate([c_prompt, c_sample], axis=0)
    return _trunk((x_prompt, x_sample), c, p)
```

```python
import functools
import math

import numpy as np
import jax
import jax.numpy as jnp
from jax import lax
from jax.experimental import pallas as pl
from jax.experimental.pallas import tpu as pltpu

D_MODEL = 1024
SEQ = 2048
DEPTH = 4
HEAD_DIM = 64
GRID_W = 64
NA_HEADS = 4
NA_WIN_H = 8
NA_WIN_W = 16
SW_HEADS = 4
SW_KV_HEADS = 2
SW_HALF_WINDOW = 128
DIL_PAIRS = ((128, 1), (512, 4), (2048, 16))
DIFF_HEADS = 4
DIFF_HALF = HEAD_DIM // 2
T5_BUCKETS = 32
T5_MAX_DIST = 128
T5_B0 = 0
T5_C0 = 4
T5_D0 = 10
D_FF = 2816
RMS_EPS = 1e-6
NEG_INF = -1e30
LOG2E = 1.4426950408889634

F32 = jnp.float32
BF16 = jnp.bfloat16

VMEM_LIMIT_BYTES = 56 * 1024 * 1024
TN_IN = 512
TN_OUT = 1024
TN_FFN = 1024
FF_CHUNK = 256
N_FF_CHUNKS = D_FF // FF_CHUNK
HALO = 8
TQ_D = 512
D_SAFE_BOUND = 60.0
KB_D = 256
D_NEAR = TQ_D // KB_D + 2
TQ_B = 128
TQ_C = 128
A_PAIR = 2 * GRID_W
A_WIN = 10 * GRID_W

QT_ROWS = 768
VT_ROWS = 640
K_ROWS = 640
C_QK_ROWS = 768
WT_ROWS = QT_ROWS + VT_ROWS + K_ROWS + C_QK_ROWS
C_WIDTH = 1152


def _cparams(sem):
    return pltpu.CompilerParams(dimension_semantics=sem, vmem_limit_bytes=VMEM_LIMIT_BYTES)


def _const_spec(shape):
    nd = len(shape)
    return pl.BlockSpec(shape, lambda *_: (0,) * nd, pipeline_mode=pl.Buffered(1))


def _ada_kernel(c_ref, w_ref, b_ref, o_ref):
    c = c_ref[...]
    sc = (c * (1.0 / (1.0 + jnp.exp(-c)))).astype(BF16)
    o_ref[0] = jnp.dot(sc, w_ref[0].astype(BF16), preferred_element_type=F32) + b_ref[0]


def _ada_call(c, w_ada, b_ada):
    bt = c.shape[0]
    ncol = 6 * D_MODEL
    cb = 1024
    return pl.pallas_call(
        _ada_kernel,
        grid=(DEPTH, ncol // cb),
        in_specs=[
            pl.BlockSpec((bt, D_MODEL), lambda l, j: (0, 0)),
            pl.BlockSpec((1, D_MODEL, cb), lambda l, j: (l, 0, j)),
            pl.BlockSpec((1, 1, cb), lambda l, j: (l, 0, j)),
        ],
        out_specs=pl.BlockSpec((1, bt, cb), lambda l, j: (l, 0, j)),
        out_shape=jax.ShapeDtypeStruct((DEPTH, bt, ncol), F32),
        compiler_params=_cparams(("arbitrary", "arbitrary")),
        name="ada_mod",
    )(c, w_ada, b_ada.reshape(DEPTH, 1, ncol))


def _mod_norm(x, g, scale, shift):
    ms = jnp.mean(x * x, axis=-1, keepdims=True)
    return x * lax.rsqrt(ms + RMS_EPS) * (g * (1.0 + scale)) + shift


def _pick_group(refs, nb_first):
    if len(refs) == 1:
        return refs[0][0]
    return jnp.where(pl.program_id(0) < nb_first, refs[0][0], refs[1][0])


def _group_specs(xs, tn):
    d = xs[0].shape[-1]
    if len(xs) == 1:
        return [pl.BlockSpec((1, tn, d), lambda b, t: (b, t, 0))]
    nb = xs[0].shape[0]
    last = xs[0].shape[1] // tn - 1
    return [pl.BlockSpec((1, tn, d), lambda b, t: (jnp.minimum(b, nb - 1), jnp.where(b < nb, t, last), 0)),
            pl.BlockSpec((1, tn, d), lambda b, t: (jnp.maximum(b - nb, 0), jnp.where(b < nb, 0, t), 0))]


def _in_kernel(*refs, nsrc, nb_first):
    x_refs, (mod_ref, g_ref, wt_ref, wvc_ref, gs_ref, qT_ref, vT_ref, k_ref, c_ref, h_scr) = \
        refs[:nsrc], refs[nsrc:]
    tn = x_refs[0].shape[1]
    nsub = 2
    ts = tn // nsub
    x = _pick_group(x_refs, nb_first)

    def norm_sub(i):
        h = _mod_norm(x[i * ts:(i + 1) * ts], g_ref[...], mod_ref[0, 1:2, :], mod_ref[0, 0:1, :])
        h_scr[i * ts:(i + 1) * ts, :] = h.astype(BF16)

    def head_norm(y, r0, grp):
        rows = y.shape[0]
        y3 = y.reshape(rows // grp, grp, ts)
        ms = jnp.sum(y3 * y3, axis=1, keepdims=True) * (1.0 / grp)
        gs = gs_ref[r0:r0 + rows, :].reshape(rows // grp, grp, 1)
        return (y3 * lax.rsqrt(ms + RMS_EPS) * gs).reshape(rows, ts)

    items = [(r0, 256, grp, ("qT", r0)) for r0, grp in ((0, HEAD_DIM), (256, HEAD_DIM), (512, DIFF_HALF))]
    items += [(QT_ROWS + r0, rows, None, ("vT", r0)) for r0, rows in ((0, 256), (256, 256), (512, 128))]
    base = QT_ROWS + VT_ROWS
    items += [(base + r0, rows, grp, ("k", r0))
              for r0, rows, grp in ((0, 256, DIFF_HALF), (256, 256, HEAD_DIM), (512, 128, HEAD_DIM))]
    base += K_ROWS
    items += [(base + r0, 256, HEAD_DIM, ("c", r0)) for r0 in (0, 256, 512)]
    items += [(None, C_WIDTH - C_QK_ROWS, None, ("c", C_QK_ROWS))]

    def matmul(item, i):
        w0, rows = item[0], item[1]
        hb = h_scr[i * ts:(i + 1) * ts, :]
        if w0 is None:
            return jnp.dot(hb, wvc_ref[...], preferred_element_type=F32)
        return lax.dot_general(wt_ref[w0:w0 + rows, :], hb, (((1,), (1,)), ((), ())),
                               preferred_element_type=F32)

    def finish(item, i, y):
        w0, rows, grp, (dst, r0) = item
        tok = slice(i * ts, (i + 1) * ts)
        if grp is not None:
            y = head_norm(y, w0, grp)
        if dst == "qT":
            qT_ref[0, r0:r0 + rows, tok] = y.astype(BF16)
        elif dst == "vT":
            vT_ref[0, r0:r0 + rows, tok] = y.astype(BF16)
        elif dst == "k":
            k_ref[0, tok, r0:r0 + rows] = y.T.astype(BF16)
        else:
            c_ref[0, tok, r0:r0 + rows] = y if w0 is None else y.T

    work = [(item, i) for i in range(nsub) for item in items]
    norm_sub(0)
    pending = None
    for n_issued, (item, i) in enumerate(work):
        if n_issued == 2:
            for j in range(1, nsub):
                norm_sub(j)
        y = matmul(item, i)
        if pending is not None:
            finish(*pending)
        pending = (item, i, y)
    finish(*pending)


def _in_call(xs, mod, g, wt, wvc, gs):
    bt = sum(x.shape[0] for x in xs)
    _, n, d = xs[0].shape
    tn = TN_IN
    return pl.pallas_call(
        functools.partial(_in_kernel, nsrc=len(xs), nb_first=xs[0].shape[0]),
        grid=(bt, n // tn),
        in_specs=_group_specs(xs, tn) + [
            pl.BlockSpec((1, 6, d), lambda b, t: (b, 0, 0)),
            _const_spec((1, d)),
            _const_spec((WT_ROWS, d)),
            _const_spec((d, C_WIDTH - C_QK_ROWS)),
            _const_spec((WT_ROWS, 1)),
        ],
        out_specs=[
            pl.BlockSpec((1, QT_ROWS, tn), lambda b, t: (b, 0, t)),
            pl.BlockSpec((1, VT_ROWS, tn), lambda b, t: (b, 0, t)),
            pl.BlockSpec((1, tn, K_ROWS), lambda b, t: (b, t, 0)),
            pl.BlockSpec((1, tn, C_WIDTH), lambda b, t: (b, t, 0)),
        ],
        out_shape=[
            jax.ShapeDtypeStruct((bt, QT_ROWS, n), BF16),
            jax.ShapeDtypeStruct((bt, VT_ROWS, n), BF16),
            jax.ShapeDtypeStruct((bt, n, K_ROWS), BF16),
            jax.ShapeDtypeStruct((bt, n, C_WIDTH), F32),
        ],
        scratch_shapes=[pltpu.VMEM((tn, d), BF16)],
        compiler_params=_cparams(("arbitrary", "arbitrary")),
        name="in_proj",
    )(*xs, mod, g, wt, wvc, gs)


def _d_kernel(bound_ref, lam_ref, sub_ref, q_ref, k_ref, v_ref, tbl_ref, o_ref, s_scr, p_scr, *, lambda_init):
    t = pl.program_id(1)
    tq = q_ref.shape[2]
    nkb = k_ref.shape[1] // KB_D
    qT = q_ref[0]
    rowgrp = lax.broadcasted_iota(jnp.int32, qT.shape, 0) // DIFF_HALF
    lp = lam_ref[...]
    lam = (jnp.exp(jnp.sum(lp[0:1] * lp[1:2], axis=-1, keepdims=True))
           - jnp.exp(jnp.sum(lp[2:3] * lp[3:4], axis=-1, keepdims=True)) + lambda_init)

    first = t * (tq // KB_D)
    blocks = []
    for d in range(-1, nkb - 1):
        jb = lax.rem(first + d + nkb, nkb)
        blocks.append((d < D_NEAR - 1, pl.multiple_of(jb * KB_D, KB_D),
                       jnp.clip(jb - first, -2, D_NEAR - 1) + 2))

    nblk = len(blocks)
    vmax = lambda s: jnp.max(jnp.max(s.reshape(KB_D // 8, 8, tq), axis=0), axis=0, keepdims=True)

    def scores(h, pos, st):
        near, k0, idx = blocks[pos]
        kb = k_ref[0, pl.ds(k0, KB_D), :]
        for i in range(2):
            slot = 2 * (h % 2) + i
            s = jnp.dot(kb, st["q"][i], preferred_element_type=F32)
            if near:
                s = s + tbl_ref[h, idx]
                s_scr[slot, pos] = s
                st["m"][i] = jnp.maximum(st["m"][i], vmax(s))
            else:
                s_scr[slot, pos] = s
                st["m"][i] = jnp.maximum(st["m"][i], vmax(s) + tbl_ref[h, idx, 0:1, :])

    def probs(h, pos, st):
        near, k0, idx = blocks[pos]
        for i in range(2):
            slot = 2 * (h % 2) + i
            shift = st["m"][i] if near else st["m"][i] - tbl_ref[h, idx, 0:1, :]
            p_scr[slot, pl.ds(k0, KB_D), :] = jnp.exp2(s_scr[slot, pos] - shift).astype(BF16)

    def new_state(h):
        return {"q": [jnp.where(rowgrp == 2 * h + i, qT, jnp.zeros_like(qT)) for i in range(2)],
                "m": [jnp.full((1, tq), -jnp.inf, F32)] * 2}

    ones = jnp.ones((16, k_ref.shape[1]), BF16)

    def finish(h, st):
        vh = jnp.concatenate([v_ref[0, h * HEAD_DIM:(h + 1) * HEAD_DIM, :], ones], axis=0)
        o = []
        for i in range(2):
            ol = jnp.dot(vh, p_scr[2 * (h % 2) + i], preferred_element_type=F32)
            o.append(ol[:HEAD_DIM] * (1.0 / ol[HEAD_DIM:HEAD_DIM + 1]))
        oh = o[0] - lam * o[1]
        ms = jnp.mean(oh * oh, axis=0, keepdims=True)
        return oh * lax.rsqrt(ms + RMS_EPS) * sub_ref[...] * (1.0 - lambda_init)

    def exact_max_path():
        st = new_state(0)
        for pos in range(nblk):
            scores(0, pos, st)
        parts = []
        for h in range(DIFF_HEADS):
            nxt = new_state(h + 1) if h + 1 < DIFF_HEADS else None
            for pos in range(nblk):
                if nxt is not None:
                    scores(h + 1, pos, nxt)
                probs(h, pos, st)
            parts.append(finish(h, st))
            st = nxt
        o_ref[0] = jnp.concatenate(parts, axis=0).T.astype(BF16)

    def bounded_path():
        bound = bound_ref[0]

        def block(h, st, pos):
            near, k0, idx = blocks[pos]
            kb = k_ref[0, pl.ds(k0, KB_D), :]
            for i in range(2):
                s = jnp.dot(kb, st["q"][i], preferred_element_type=F32)
                if near:
                    p = jnp.exp2(s + tbl_ref[h, idx] - bound)
                else:
                    p = jnp.exp2(s - (bound - tbl_ref[h, idx, 0:1, :]))
                p_scr[2 * (h % 2) + i, pl.ds(k0, KB_D), :] = p.astype(BF16)

        lag = 3
        parts = []
        states = [new_state(h) for h in range(DIFF_HEADS)]
        for h in range(DIFF_HEADS):
            for pos in range(nblk):
                block(h, states[h], pos)
                if h > 0 and pos == lag - 1:
                    parts.append(finish(h - 1, states[h - 1]))
        parts.append(finish(DIFF_HEADS - 1, states[-1]))
        o_ref[0] = jnp.concatenate(parts, axis=0).T.astype(BF16)

    pl.when(bound_ref[0] <= D_SAFE_BOUND)(bounded_path)
    pl.when(bound_ref[0] > D_SAFE_BOUND)(exact_max_path)


def _d_score_bound(qkn_d_l, t5_table):
    gq = jnp.max(jnp.abs(qkn_d_l[0])) * (DIFF_HALF ** -0.5 * LOG2E)
    gk = jnp.max(jnp.abs(qkn_d_l[1]))
    bias = jnp.max(jnp.abs(t5_table[:, T5_D0:T5_D0 + DIFF_HEADS])) * LOG2E
    return (1.02 * DIFF_HALF * gq * gk + bias).astype(F32).reshape(1)


def _d_call(bound, lam_p, subln, qT, k, vT, tbl, lambda_init):
    bt, _, n = qT.shape
    tq = TQ_D
    return pl.pallas_call(
        functools.partial(_d_kernel, lambda_init=lambda_init),
        grid=(bt, n // tq),
        in_specs=[
            pl.BlockSpec(memory_space=pltpu.SMEM),
            _const_spec((4, DIFF_HALF)),
            _const_spec((HEAD_DIM, 1)),
            pl.BlockSpec((1, 256, tq), lambda b, t: (b, 2, t)),
            pl.BlockSpec((1, n, 256), lambda b, t: (b, 0, 0)),
            pl.BlockSpec((1, 256, n), lambda b, t: (b, 0, 0)),
            _const_spec((DIFF_HEADS, D_NEAR + 2, KB_D, tq)),
        ],
        out_specs=pl.BlockSpec((1, tq, 256), lambda b, t: (b, t, 0)),
        out_shape=jax.ShapeDtypeStruct((bt, n, 256), BF16),
        scratch_shapes=[
            pltpu.VMEM((4, n // KB_D, KB_D, tq), F32),
            pltpu.VMEM((4, n, tq), BF16),
        ],
        compiler_params=_cparams(("parallel", "arbitrary")),
        name="mixer_d",
    )(bound, lam_p, subln, qT, k, vT, tbl)


def _b_kernel(sink_ref, q_ref, k_ref, v_ref, bias_ref, o_ref):
    n = k_ref.shape[1]
    tq = TQ_B
    nt = n // tq
    grp = SW_HEADS // SW_KV_HEADS
    zero = jnp.zeros((HEAD_DIM, tq), BF16)
    sinks = [sink_ref[h] * LOG2E for h in range(SW_HEADS)]
    group = 4
    for t0 in range(0, nt, group):
        chains = []
        for t in range(t0, t0 + group):
            qT = q_ref[0, :, t * tq:(t + 1) * tq]
            qcat = []
            for h in range(SW_HEADS):
                qh = qT[h * HEAD_DIM:(h + 1) * HEAD_DIM]
                qcat.append(jnp.concatenate([qh, zero] if h // grp == 0 else [zero, qh], axis=0))
            qcat = jnp.concatenate(qcat, axis=1)
            blocks = [j for j in (-1, 0, 1) if 0 <= t + j < nt]
            s_all = [jnp.dot(k_ref[0, (t + j) * tq:(t + j + 1) * tq, :], qcat,
                             preferred_element_type=F32) for j in blocks]
            for h in range(SW_HEADS):
                chains.append((t, h, [(s[:, h * tq:(h + 1) * tq] + bias_ref[h, j + 1], t + j)
                                      for s, j in zip(s_all, blocks)]))
        m_l = []
        for t, h, ss in chains:
            m = jnp.full((1, tq), sinks[h], F32)
            for s, _ in ss:
                m = jnp.maximum(m, jnp.max(s, axis=0, keepdims=True))
            m_l.append(m)
        p_l = [[jnp.exp2(s - m) for s, _ in ss] for (t, h, ss), m in zip(chains, m_l)]
        outs = []
        for (t, h, ss), m, ps in zip(chains, m_l, p_l):
            kv = h // grp
            l = jnp.exp2(sinks[h] - m)
            o = jnp.zeros((HEAD_DIM, tq), F32)
            for p, (_, kb) in zip(ps, ss):
                l = l + jnp.sum(p, axis=0, keepdims=True)
                vh = v_ref[0, kv * HEAD_DIM:(kv + 1) * HEAD_DIM, kb * tq:(kb + 1) * tq]
                o = o + jnp.dot(vh, p.astype(BF16), preferred_element_type=F32)
            outs.append(o * (1.0 / l))
        for i, t in enumerate(range(t0, t0 + group)):
            o_ref[0, t * tq:(t + 1) * tq, :] = jnp.concatenate(
                outs[i * SW_HEADS:(i + 1) * SW_HEADS], axis=0).T.astype(BF16)


def _b_call(sink, qT, k, vT, bias):
    bt, _, n = qT.shape
    return pl.pallas_call(
        _b_kernel,
        grid=(bt,),
        in_specs=[
            pl.BlockSpec(memory_space=pltpu.SMEM),
            pl.BlockSpec((1, 256, n), lambda b: (b, 1, 0)),
            pl.BlockSpec((1, n, 128), lambda b: (b, 0, 4)),
            pl.BlockSpec((1, 128, n), lambda b: (b, 4, 0)),
            _const_spec((SW_HEADS, 3, TQ_B, TQ_B)),
        ],
        out_specs=pl.BlockSpec((1, n, 256), lambda b: (b, 0, 0)),
        out_shape=jax.ShapeDtypeStruct((bt, n, 256), BF16),
        compiler_params=_cparams(("parallel",)),
        name="mixer_b",
    )(sink, qT, k, vT, bias)


def _a_kernel(q_ref, k_ref, v_ref, bias_ref, o_ref):
    n = k_ref.shape[1]
    rows = n // GRID_W
    npair = n // A_PAIR
    kh = min(NA_WIN_H, rows)
    rowhead = lax.broadcasted_iota(jnp.int32, (NA_HEADS * HEAD_DIM, A_PAIR), 0) // HEAD_DIM
    group = 8
    for g0 in range(0, npair, group):
        geo, s_all = [], []
        for rp in range(g0, g0 + group):
            r = 2 * rp
            ws = min(max(r - kh // 2, 0), rows - kh, rows - A_WIN // GRID_W)
            var = rp if rp < 2 else (2 if rp < npair - 2 else rp - (npair - 5))
            q0, k0 = rp * A_PAIR, ws * GRID_W
            qT = q_ref[0, :, q0:q0 + A_PAIR]
            qcat = jnp.concatenate([jnp.where(rowhead == h, qT, jnp.zeros_like(qT))
                                    for h in range(NA_HEADS)], axis=1)
            s_all.append(jnp.dot(k_ref[0, k0:k0 + A_WIN, :], qcat, preferred_element_type=F32))
            geo.append((q0, k0, var))
        chains = [(i, h) for i in range(group) for h in range(NA_HEADS)]
        s_l = [s_all[i][:, h * A_PAIR:(h + 1) * A_PAIR] + bias_ref[h, geo[i][2]] for i, h in chains]
        m_l = [jnp.max(s, axis=0, keepdims=True) for s in s_l]
        p_l = [jnp.exp2(s - m) for s, m in zip(s_l, m_l)]
        l_l = [jnp.sum(p, axis=0, keepdims=True) for p in p_l]
        o_l = [jnp.dot(v_ref[0, h * HEAD_DIM:(h + 1) * HEAD_DIM, geo[i][1]:geo[i][1] + A_WIN],
                       p.astype(BF16), preferred_element_type=F32) * (1.0 / l)
               for (i, h), p, l in zip(chains, p_l, l_l)]
        for i in range(group):
            q0 = geo[i][0]
            o_ref[0, q0:q0 + A_PAIR, :] = jnp.concatenate(
                o_l[i * NA_HEADS:(i + 1) * NA_HEADS], axis=0).T.astype(BF16)


def _a_call(qT, k, vT, bias, layer):
    bt, _, n = qT.shape
    return pl.pallas_call(
        _a_kernel,
        grid=(bt,),
        in_specs=[
            pl.BlockSpec((1, 256, n), lambda b: (b, 0, 0)),
            pl.BlockSpec((1, n, 256), lambda b: (b, 0, 1)),
            pl.BlockSpec((1, 256, n), lambda b: (b, 1, 0)),
            pl.BlockSpec((NA_HEADS, 5, A_WIN, A_PAIR), lambda b: (layer, 0, 0, 0),
                         pipeline_mode=pl.Buffered(1)),
        ],
        out_specs=pl.BlockSpec((1, n, 256), lambda b: (b, 0, 0)),
        out_shape=jax.ShapeDtypeStruct((bt, n, 256), BF16),
        compiler_params=_cparams(("parallel",)),
        name="mixer_a",
    )(qT, k, vT, bias)


def _c_kernel(q_ref, k_ref, v_ref, bias_ref, o_ref, lse_ref, sub_scr, o_scr, lse_scr, *, dil):
    n = q_ref.shape[1]
    sub_len = n // dil
    win = min(2 * TQ_C, sub_len)
    nqb = sub_len // TQ_C
    lane_head = lax.broadcasted_iota(jnp.int32, (TQ_C, 2 * HEAD_DIM), 1) // HEAD_DIM

    def gather(p, carry):
        src = pl.ds(p, sub_len, stride=dil)
        dst = pl.ds(pl.multiple_of(p * sub_len, sub_len), sub_len)
        for i, ref in enumerate((q_ref, k_ref, v_ref)):
            sub_scr[i, dst, :] = ref[0, src, :].astype(BF16)
        return carry

    if dil > 1:
        lax.fori_loop(0, dil, gather, 0)
    else:
        for i, ref in enumerate((q_ref, k_ref, v_ref)):
            sub_scr[i] = ref[0].astype(BF16)

    group = 4
    for g0 in range(0, n // TQ_C, group):
        geo = []
        for item in range(g0, g0 + group):
            p, qb = divmod(item, nqb)
            q0 = p * sub_len + qb * TQ_C
            start = p * sub_len + min(max(qb * TQ_C - TQ_C // 2, 0), sub_len - win)
            var = 0 if qb == 0 else (2 if qb == nqb - 1 else 1)
            geo.extend((q0, start, var, j) for j in range(2))
        s_l = []
        for q0, start, var, j in geo:
            qblk = sub_scr[0, q0:q0 + TQ_C, :]
            qm = jnp.where(lane_head == j, qblk, jnp.zeros_like(qblk))
            s = lax.dot_general(qm, sub_scr[1, start:start + win, :], (((1,), (1,)), ((), ())),
                                preferred_element_type=F32)
            s_l.append(s + bias_ref[j, var])
        m_l = [jnp.max(s, axis=-1, keepdims=True) for s in s_l]
        p_l = [jnp.exp2(s - m) for s, m in zip(s_l, m_l)]
        l_l = [jnp.sum(pr, axis=-1, keepdims=True) for pr in p_l]
        o_l = [jnp.dot(pr.astype(BF16), sub_scr[2, start:start + win, :], preferred_element_type=F32)
               for pr, (q0, start, var, j) in zip(p_l, geo)]
        for i in range(0, len(geo), 2):
            q0 = geo[i][0]
            o0, o1 = o_l[i] * (1.0 / l_l[i]), o_l[i + 1] * (1.0 / l_l[i + 1])
            e0 = jnp.broadcast_to(m_l[i] + jnp.log2(l_l[i]), (TQ_C, 2 * HEAD_DIM))
            e1 = jnp.broadcast_to(m_l[i + 1] + jnp.log2(l_l[i + 1]), (TQ_C, 2 * HEAD_DIM))
            o_scr[q0:q0 + TQ_C, :] = jnp.where(lane_head == 0, o0, o1)
            lse_scr[q0:q0 + TQ_C, :] = jnp.where(lane_head == 0, e0, e1)

    def scatter(p, carry):
        dst = pl.ds(p, sub_len, stride=dil)
        src = pl.ds(pl.multiple_of(p * sub_len, sub_len), sub_len)
        o_ref[0, dst, :] = o_scr[src, :]
        lse_ref[0, dst, :] = lse_scr[src, :]
        return carry

    if dil > 1:
        lax.fori_loop(0, dil, scatter, 0)
    else:
        o_ref[0] = o_scr[...]
        lse_ref[0] = lse_scr[...]


def _c_call(qkv, bias, g, dil):
    bt, n, _ = qkv.shape
    sub_len = n // dil
    win = min(2 * TQ_C, sub_len)
    lanes = 2 * HEAD_DIM
    return pl.pallas_call(
        functools.partial(_c_kernel, dil=dil),
        grid=(bt,),
        in_specs=[
            pl.BlockSpec((1, n, lanes), lambda b: (b, 0, g)),
            pl.BlockSpec((1, n, lanes), lambda b: (b, 0, 3 + g)),
            pl.BlockSpec((1, n, lanes), lambda b: (b, 0, 6 + g)),
            _const_spec((2, 3, TQ_C, win)),
        ],
        out_specs=[
            pl.BlockSpec((1, n, lanes), lambda b: (b, 0, 0)),
            pl.BlockSpec((1, n, lanes), lambda b: (b, 0, 0)),
        ],
        out_shape=[
            jax.ShapeDtypeStruct((bt, n, lanes), F32),
            jax.ShapeDtypeStruct((bt, n, lanes), F32),
        ],
        scratch_shapes=[
            pltpu.VMEM((3, n, lanes), BF16),
            pltpu.VMEM((n, lanes), F32),
            pltpu.VMEM((n, lanes), F32),
        ],
        compiler_params=_cparams(("parallel",)),
        name=f"mixer_c{g}",
    )(qkv, qkv, qkv, bias)


def _out_kernel(*refs, nsrc, nb_first):
    x_refs, (mod_ref, oa_ref, ob_ref, od_ref, oc0_ref, oc1_ref, oc2_ref,
             l0_ref, l1_ref, l2_ref, w_ref, o_ref) = refs[:nsrc], refs[nsrc:]
    l0, l1, l2 = l0_ref[0], l1_ref[0], l2_ref[0]
    mx = jnp.maximum(jnp.maximum(l0, l1), l2)
    e0, e1, e2 = jnp.exp2(l0 - mx), jnp.exp2(l1 - mx), jnp.exp2(l2 - mx)
    inv = 1.0 / (e0 + e1 + e2)
    mixed = jnp.concatenate([oa_ref[0], ob_ref[0],
                             (oc0_ref[0] * (e0 * inv)).astype(BF16),
                             (oc1_ref[0] * (e1 * inv)).astype(BF16),
                             (oc2_ref[0] * (e2 * inv)).astype(BF16), od_ref[0]], axis=1)
    y = jnp.dot(mixed, w_ref[...], preferred_element_type=F32)
    o_ref[0] = _pick_group(x_refs, nb_first) + mod_ref[0, 2:3, :] * y


def _out_call(xs, mod, o_a, o_b, o_d, o_c, lse_c, w_out):
    bt = sum(x.shape[0] for x in xs)
    _, n, d = xs[0].shape
    tn = TN_OUT
    tok = lambda w: pl.BlockSpec((1, tn, w), lambda b, t: (b, t, 0))
    return pl.pallas_call(
        functools.partial(_out_kernel, nsrc=len(xs), nb_first=xs[0].shape[0]),
        grid=(bt, n // tn),
        in_specs=_group_specs(xs, tn) + [pl.BlockSpec((1, 6, d), lambda b, t: (b, 0, 0)),
                  tok(256), tok(256), tok(256), tok(128), tok(128), tok(128),
                  tok(128), tok(128), tok(128), _const_spec(w_out.shape)],
        out_specs=tok(d),
        out_shape=jax.ShapeDtypeStruct((bt, n, d), F32),
        compiler_params=_cparams(("arbitrary", "arbitrary")),
        name="out_proj",
    )(*xs, mod, o_a, o_b, o_d, *o_c, *lse_c, w_out)


def _ffn_kernel(xp_ref, x_ref, xn_ref, mod_ref, g_ref, wup_ref, cw_ref, wdn_ref, *refs, nb_first):
    o_refs, (h_scr, acc_scr, act_scr) = refs[:-3], refs[-3:]
    t = pl.program_id(1)
    nt = pl.num_programs(1)
    tn = x_ref.shape[1]
    ext = tn + 2 * HALO
    xe = jnp.concatenate([xp_ref[0], x_ref[0], xn_ref[0]], axis=0)
    h = _mod_norm(xe, g_ref[...], mod_ref[0, 4:5, :], mod_ref[0, 3:4, :])
    row = lax.broadcasted_iota(jnp.int32, (ext, 1), 0)
    keep = jnp.logical_and(jnp.logical_or(row >= HALO, t > 0),
                           jnp.logical_or(row < HALO + tn, t < nt - 1))
    h_scr[...] = jnp.where(keep, h, 0.0).astype(BF16)

    nsplit = 4
    rows = tn // nsplit

    def conv(u, cw, r0):
        ue = u[r0:r0 + rows + 2 * HALO]
        prev = pltpu.roll(ue, 1, 0)[HALO:HALO + rows]
        nxt = pltpu.roll(ue, rows + 2 * HALO - 1, 0)[HALO:HALO + rows]
        return prev * cw[0:1] + ue[HALO:HALO + rows] * cw[1:2] + nxt * cw[2:3] + cw[3:4]

    hb = h_scr[...]
    up = lambda c: jnp.dot(hb, wup_ref[c], preferred_element_type=F32)
    u_val, u_gate = up(0), up(N_FF_CHUNKS)
    kgroup = 6
    for c in range(N_FF_CHUNKS):
        nxt_u = []
        g0 = c - c % kgroup
        gl = min(kgroup, N_FF_CHUNKS - g0)
        last_of_group = c == g0 + gl - 1
        col = (c - g0) * FF_CHUNK
        for part in range(nsplit):
            r0 = part * rows
            if c + 1 < N_FF_CHUNKS and part % (nsplit // 2) == 0:
                nxt_u.append(up(c + 1 + len(nxt_u) * N_FF_CHUNKS))
            val = conv(u_val, cw_ref[c], r0)
            gate = conv(u_gate, cw_ref[N_FF_CHUNKS + c], r0)
            act_scr[r0:r0 + rows, col:col + FF_CHUNK] = (val * gate * jax.nn.sigmoid(gate)).astype(BF16)
            if last_of_group:
                wd = wdn_ref[g0:g0 + gl].reshape(gl * FF_CHUNK, wdn_ref.shape[2])
                y = jnp.dot(act_scr[r0:r0 + rows, 0:gl * FF_CHUNK], wd, preferred_element_type=F32)
                if g0 == 0:
                    acc_scr[r0:r0 + rows, :] = y
                else:
                    acc_scr[r0:r0 + rows, :] += y
        if nxt_u:
            u_val, u_gate = nxt_u
    if len(o_refs) == 1:
        o_refs[0][0] = x_ref[0] + mod_ref[0, 5:6, :] * acc_scr[...]
    else:
        first = pl.program_id(0) < nb_first

        @pl.when(first)
        def _():
            o_refs[0][0] = x_ref[0] + mod_ref[0, 5:6, :] * acc_scr[...]

        @pl.when(jnp.logical_not(first))
        def _():
            o_refs[1][0] = x_ref[0] + mod_ref[0, 5:6, :] * acc_scr[...]


def _ffn_call(x, mod, g, wup, cw, wdn, nb_first=None):
    bt, n, d = x.shape
    tn = TN_FFN
    nb = tn // HALO
    nhalo = n // HALO
    if nb_first is None:
        out_specs = pl.BlockSpec((1, tn, d), lambda b, t: (b, t, 0))
        out_shape = jax.ShapeDtypeStruct((bt, n, d), F32)
    else:
        last = n // tn - 1
        out_specs = [
            pl.BlockSpec((1, tn, d), lambda b, t: (jnp.minimum(b, nb_first - 1),
                                                   jnp.where(b < nb_first, t, last), 0)),
            pl.BlockSpec((1, tn, d), lambda b, t: (jnp.maximum(b - nb_first, 0),
                                                   jnp.where(b < nb_first, 0, t), 0))]
        out_shape = [jax.ShapeDtypeStruct((nb_first, n, d), F32),
                     jax.ShapeDtypeStruct((bt - nb_first, n, d), F32)]
    return pl.pallas_call(
        functools.partial(_ffn_kernel, nb_first=nb_first),
        grid=(bt, n // tn),
        in_specs=[
            pl.BlockSpec((1, HALO, d), lambda b, t: (b, jnp.maximum(t * nb - 1, 0), 0)),
            pl.BlockSpec((1, tn, d), lambda b, t: (b, t, 0)),
            pl.BlockSpec((1, HALO, d), lambda b, t: (b, jnp.minimum((t + 1) * nb, nhalo - 1), 0)),
            pl.BlockSpec((1, 6, d), lambda b, t: (b, 0, 0)),
            _const_spec((1, d)),
            _const_spec(wup.shape),
            _const_spec(cw.shape),
            _const_spec(wdn.shape),
        ],
        out_specs=out_specs,
        out_shape=out_shape,
        scratch_shapes=[pltpu.VMEM((tn + 2 * HALO, d), BF16), pltpu.VMEM((tn, d), F32),
                        pltpu.VMEM((tn, 6 * FF_CHUNK), BF16)],
        compiler_params=_cparams(("arbitrary", "arbitrary")),
        name="ffn",
    )(x, x, x, mod, g, wup, cw, wdn)


def _t5_bucket(rel):
    nb = T5_BUCKETS // 2
    max_exact = nb // 2
    base = jnp.where(rel > 0, nb, 0)
    n = jnp.abs(rel)
    nf = jnp.maximum(n, 1).astype(jnp.float32)
    large = max_exact + (jnp.log(nf / max_exact) / math.log(T5_MAX_DIST / max_exact)
                         * (nb - max_exact)).astype(jnp.int32)
    large = jnp.minimum(large, nb - 1)
    return base + jnp.where(n < max_exact, n, large)


def _t5_kernel(tab_ref, bkt_ref, o_ref, *, head0):
    h = head0 + pl.program_id(0)
    for v in range(bkt_ref.shape[0]):
        bkt = bkt_ref[v]
        acc = jnp.full(bkt.shape, NEG_INF, F32)
        for b in range(T5_BUCKETS):
            acc = jnp.where(bkt == b, tab_ref[b, h] * LOG2E, acc)
        o_ref[0, v] = acc


def _t5_bias(table, rel, mask, head0, nheads):
    bkt = jnp.where(jnp.asarray(mask), _t5_bucket(jnp.asarray(rel, dtype=jnp.int32)), -1)
    return pl.pallas_call(
        functools.partial(_t5_kernel, head0=head0),
        grid=(nheads,),
        in_specs=[pl.BlockSpec(memory_space=pltpu.SMEM), _const_spec(bkt.shape)],
        out_specs=pl.BlockSpec((1,) + bkt.shape, lambda h: (h, 0, 0, 0)),
        out_shape=jax.ShapeDtypeStruct((nheads,) + bkt.shape, F32),
        compiler_params=_cparams(("arbitrary",)),
        name="t5_bias",
    )(table.astype(F32), bkt)


def _bias_tables_t5(t5_table):
    kk = np.arange(KB_D)[:, None]
    qq = np.arange(TQ_D)[None, :]
    far = 8 * T5_MAX_DIST
    rel_d = np.stack([np.full((KB_D, TQ_D), -far)]
                     + [j * KB_D + kk - qq for j in range(-1, D_NEAR - 1)]
                     + [np.full((KB_D, TQ_D), far)])
    tbl_d = _t5_bias(t5_table, rel_d, np.ones_like(rel_d, bool), T5_D0, DIFF_HEADS)
    kk = np.arange(TQ_B)[:, None]
    qq = np.arange(TQ_B)[None, :]
    rel_b = np.stack([j * TQ_B + kk - qq for j in (-1, 0, 1)])
    tbl_b = _t5_bias(t5_table, rel_b, np.abs(rel_b) <= SW_HALF_WINDOW, T5_B0, SW_HEADS)
    tbl_c = []
    for g, (w, r) in enumerate(DIL_PAIRS):
        sub_len = SEQ // r
        hw = w // (2 * r)
        win = min(2 * TQ_C, sub_len)
        qq = np.arange(TQ_C)[:, None]
        kk = np.arange(win)[None, :]
        offs = (0, -(TQ_C // 2), -(win - TQ_C))
        rel_c = np.stack([o + kk - qq for o in offs])
        tbl_c.append(_t5_bias(t5_table, rel_c * r, np.abs(rel_c) <= hw, T5_C0 + 2 * g, 2))
    return tbl_d, tbl_b, tbl_c


def _a_variants():
    rows = SEQ // GRID_W
    kh = min(NA_WIN_H, rows)
    wrows = A_WIN // GRID_W
    out = []
    for r in (0, 2, 4, rows - 4, rows - 2):
        ws = min(int(np.clip(r - kh // 2, 0, rows - kh)), rows - wrows)
        per_row = []
        for kr in range(wrows):
            pair = []
            for a in range(2):
                rsq = int(np.clip(r + a - kh // 2, 0, rows - kh))
                krow = ws + kr
                pair.append(krow - (r + a) + NA_WIN_H - 1 if rsq <= krow < rsq + kh else None)
            per_row.append(tuple(pair))
        out.append(tuple(per_row))
    return tuple(out)


def _abias_kernel(rpb_ref, dcm_ref, o_ref):
    i = pl.program_id(0)
    ncol = 2 * NA_WIN_W - 1
    dcm = dcm_ref[...]
    second = lax.broadcasted_iota(jnp.int32, dcm.shape, 1) >= GRID_W
    neg = jnp.full(dcm.shape, NEG_INF, F32)
    where = {}
    for v, per_row in enumerate(_a_variants()):
        for kr, pair in enumerate(per_row):
            where.setdefault(pair, []).append((v, kr))
    for (dr0, dr1), places in where.items():
        if dr0 is None and dr1 is None:
            tile = neg
        else:
            def body(t, acc, dr0=dr0, dr1=dr1):
                s0 = NEG_INF if dr0 is None else rpb_ref[i, dr0 * ncol + t] * LOG2E
                s1 = NEG_INF if dr1 is None else rpb_ref[i, dr1 * ncol + t] * LOG2E
                return jnp.where(dcm == t, jnp.where(second, s1, s0), acc)

            tile = lax.fori_loop(0, ncol, body, neg)
        for v, kr in places:
            o_ref[0, v, kr * GRID_W:(kr + 1) * GRID_W, :] = tile


def _bias_tables_a(rpb_a):
    nl, nh, nr, nc = rpb_a.shape
    kc = np.arange(GRID_W)[:, None]
    c = np.tile(np.arange(GRID_W), 2)[None, :]
    cs = np.clip(c - NA_WIN_W // 2, 0, GRID_W - NA_WIN_W)
    dcm = np.where((kc >= cs) & (kc < cs + NA_WIN_W), np.clip(kc - c + NA_WIN_W - 1, 0, nc - 1), -1)
    return pl.pallas_call(
        _abias_kernel,
        grid=(nl * nh,),
        in_specs=[pl.BlockSpec(memory_space=pltpu.SMEM), _const_spec(dcm.shape)],
        out_specs=pl.BlockSpec((1, 5, A_WIN, A_PAIR), lambda i: (i, 0, 0, 0)),
        out_shape=jax.ShapeDtypeStruct((nl * nh, 5, A_WIN, A_PAIR), F32),
        compiler_params=_cparams(("arbitrary",)),
        name="rpb_bias",
    )(rpb_a.reshape(nl * nh, nr * nc).astype(F32), jnp.asarray(dcm, jnp.int32))


def _in_weights(w_in_l, qkn_a, qkn_b, qkn_c, qkn_d):
    a0, b0, c0, d0 = 0, 768, 1280, 2432
    segs = ((a0, 256), (b0, 256), (d0, 256),
            (d0 + 512, 256), (a0 + 512, 256), (b0 + 384, 128),
            (d0 + 256, 256), (a0 + 256, 256), (b0 + 256, 128),
            (c0, 768))
    wt = jnp.concatenate([w_in_l[:, s0:s0 + w] for s0, w in segs], axis=1).T.astype(BF16)
    wvc = w_in_l[:, c0 + 768:c0 + 1152].astype(BF16)
    qs64 = HEAD_DIM ** -0.5 * LOG2E
    qs32 = DIFF_HALF ** -0.5 * LOG2E
    ones = jnp.ones((VT_ROWS,), F32)
    gs = jnp.concatenate([
        jnp.tile(qkn_a[0] * qs64, 4), jnp.tile(qkn_b[0] * qs64, 4), jnp.tile(qkn_d[0] * qs32, 8),
        ones,
        jnp.tile(qkn_d[1], 8), jnp.tile(qkn_a[1], 4), jnp.tile(qkn_b[1], 2),
        jnp.tile(qkn_c[0] * qs64, 6), jnp.tile(qkn_c[1], 6),
    ]).astype(F32).reshape(WT_ROWS, 1)
    return wt, wvc, gs


def _ffn_weights(w_up_l, conv_w_l, conv_b_l, w_down_l):
    d = w_up_l.shape[0]
    wup = w_up_l.reshape(d, 2 * N_FF_CHUNKS, FF_CHUNK).transpose(1, 0, 2).astype(BF16)
    cw = jnp.concatenate([conv_w_l, conv_b_l[None], jnp.zeros((4, 2 * D_FF), F32)], axis=0)
    cw = cw.reshape(8, 2 * N_FF_CHUNKS, FF_CHUNK).transpose(1, 0, 2)
    wdn = w_down_l.reshape(N_FF_CHUNKS, FF_CHUNK, d).astype(BF16)
    return wup, cw, wdn


def _layer(xs, mod, l, p, tables, split_out=None):
    tbl_a, tbl_d, tbl_b, tbl_c = tables
    wt, wvc, gs = _in_weights(p["w_in"][l], p["qkn_a"][l], p["qkn_b"][l], p["qkn_c"][l], p["qkn_d"][l])
    qT, vT, k, qkv_c = _in_call(xs, mod, p["norm_attn_g"][l].reshape(1, -1), wt, wvc, gs)
    lambda_init = 0.8 - 0.6 * math.exp(-0.3 * l)
    o_a = _a_call(qT, k, vT, tbl_a, l)
    o_b = _b_call(p["sink_b"][l], qT, k, vT, tbl_b)
    o_c, lse_c = [], []
    for g, (_, r) in enumerate(DIL_PAIRS):
        o, lse = _c_call(qkv_c, tbl_c[g], g, r)
        o_c.append(o)
        lse_c.append(lse)
    o_d = _d_call(_d_score_bound(p["qkn_d"][l], p["t5_table"]), p["lam_d"][l],
                  p["subln_d"][l].reshape(HEAD_DIM, 1), qT, k, vT, tbl_d, lambda_init)
    x = _out_call(xs, mod, o_a, o_b, o_d, o_c, lse_c, p["w_out"][l].astype(BF16))
    wup, cw, wdn = _ffn_weights(p["w_up"][l], p["conv_w"][l], p["conv_b"][l], p["w_down"][l])
    return _ffn_call(x, mod, p["norm_ffn_g"][l].reshape(1, -1), wup, cw, wdn, split_out)


def _trunk(xs, c, p, depth=DEPTH):
    bt, d = c.shape
    mod_all = _ada_call(c, p["w_ada"], p["b_ada"]).reshape(DEPTH, bt, 6, d)
    tables = (_bias_tables_a(p["rpb_a"]),) + _bias_tables_t5(p["t5_table"])
    split = xs[0].shape[0] if len(xs) == 2 else None
    x = list(xs)
    for l in range(depth):
        x = _layer(x, mod_all[l], l, p, tables, split if l == depth - 1 else None)
        x = list(x) if isinstance(x, (list, tuple)) else [x]
    return tuple(x)


def kernel(x_prompt, x_sample, c_prompt, c_sample, norm_attn_g, norm_ffn_g, w_ada, b_ada, w_in, qkn_a, qkn_b, qkn_c, qkn_d, rpb_a, sink_b, t5_table, lam_d, subln_d, w_out, w_up, conv_w, conv_b, w_down):
    p = dict(norm_attn_g=norm_attn_g, norm_ffn_g=norm_ffn_g, w_ada=w_ada, b_ada=b_ada, w_in=w_in,
             qkn_a=qkn_a, qkn_b=qkn_b, qkn_c=qkn_c, qkn_d=qkn_d, rpb_a=rpb_a, sink_b=sink_b,
             t5_table=t5_table, lam_d=lam_d, subln_d=subln_d, w_out=w_out, w_up=w_up,
             conv_w=conv_w, conv_b=conv_b, w_down=w_down)
    c = jnp.concatenate([c_prompt, c_sample], axis=0)
    return _trunk((x_prompt, x_sample), c, p)
```
